```python
import math
import jax, jax.numpy as jnp
from jax import lax
import numpy as np

D_MODEL = 1024
BATCH = 8
SEQ = 4096
DEPTH = 4
DEC_BATCH = 8
DEC_SEQ = 8192
PAST_LEN = 128

N_META = 16
N_HEADS = 8
HEAD_DIM = 64
V_DIM = 2 * HEAD_DIM
QK_WIDTH = N_HEADS * 2 * HEAD_DIM
ATTN_WIDTH = N_HEADS * V_DIM
LRU_WIDTH = D_MODEL
LRU_BLOCKS = 16
LRU_BLOCK_W = LRU_WIDTH // LRU_BLOCKS
CONV_W = 4
LRU_C = 8.0
D_FF = 3 * D_MODEL
N_EXPERTS = 8
TOP_K = 2
D_FF_EXPERT = D_FF // TOP_K
Q_BLOCK = 128
EPS = 1e-6
IN_WIDTH = 2 * QK_WIDTH + ATTN_WIDTH + 2 * LRU_WIDTH + 2 * D_MODEL
N_DENSE = (DEPTH + 1) // 2
N_MOE = DEPTH // 2

kernel_name = 'hybrid_diffattn_rglru_moe_encoder'


def _rmsnorm(x, g):
    x32 = x.astype(jnp.float32)
    y = x32 * lax.rsqrt(jnp.mean(x32 * x32, axis=-1, keepdims=True) + EPS)
    return (y * g.astype(jnp.float32)).astype(x.dtype)


def _alibi_slopes():
    return jnp.exp2(-8.0 * jnp.arange(1, N_HEADS + 1, dtype=jnp.float32) / N_HEADS)


def _diff_attn_block(qb, pos_q, k, v, pos_k, slopes, lam):
    s = jnp.einsum('bqhcd,bkhcd->bhcqk', qb, k).astype(jnp.float32) * (1.0 / math.sqrt(HEAD_DIM))
    dist = jnp.abs(pos_q[:, None] - pos_k[None, :]).astype(jnp.float32)
    s = s - slopes[None, :, None, None, None] * dist[None, None, None]
    p = jax.nn.softmax(s, axis=-1)
    p_diff = p[:, :, 0] - lam * p[:, :, 1]
    return jnp.einsum('bhqk,bkhe->bqhe', p_diff.astype(v.dtype), v)


def _diff_attention(q, k, v, slopes, lam):
    B, L = q.shape[0], q.shape[1]
    n_real = L - N_META
    n_blk = n_real // Q_BLOCK
    pos = jnp.arange(L, dtype=jnp.int32)
    out_meta = _diff_attn_block(q[:, :N_META], pos[:N_META], k, v, pos, slopes, lam)
    qr = jnp.moveaxis(q[:, N_META:].reshape(B, n_blk, Q_BLOCK, N_HEADS, 2, HEAD_DIM), 1, 0)
    pr = pos[N_META:].reshape(n_blk, Q_BLOCK)
    out_real = lax.map(lambda a: _diff_attn_block(a[0], a[1], k, v, pos, slopes, lam), (qr, pr))
    out_real = jnp.moveaxis(out_real, 0, 1).reshape(B, n_real, N_HEADS, V_DIM)
    return jnp.concatenate([out_meta, out_real], axis=1)


def _centered_dwconv(x, w, b):
    L = x.shape[1]
    left = CONV_W // 2
    right = CONV_W - 1 - left
    xp = jnp.pad(x, ((0, 0), (left, right), (0, 0)))
    y = xp[:, 0:L] * w[0]
    for j in range(1, CONV_W):
        y = y + xp[:, j:j + L] * w[j]
    return y + b


def _blockdiag(x, w, b):
    B, L, C = x.shape
    y = jnp.einsum('blnc,ncd->blnd', x.reshape(B, L, LRU_BLOCKS, LRU_BLOCK_W), w)
    return y.reshape(B, L, C) + b


def _lin_combine(c1, c2):
    a1, b1 = c1
    a2, b2 = c2
    return a1 * a2, a2 * b1 + b2


def _rglru_dir(x, wa, ba, wx, bx, lam, reverse):
    r = jax.nn.sigmoid(_blockdiag(x, wa, ba)).astype(jnp.float32)
    i = jax.nn.sigmoid(_blockdiag(x, wx, bx))
    log_a = (-LRU_C * r) * jax.nn.softplus(-lam.astype(jnp.float32))
    a = jnp.exp(log_a)
    mult = jnp.sqrt(-jnp.expm1(2.0 * log_a))
    u = mult * (i * x).astype(jnp.float32)
    _, h = lax.associative_scan(_lin_combine, (a, u), reverse=reverse, axis=1)
    return h


def _swiglu(x, wg, wu, wd):
    return (jax.nn.silu(x @ wg) * (x @ wu)) @ wd


def _moe(x, w_router, wg, wu, wd):
    logits = (x @ w_router).astype(jnp.float32)
    top_v, top_i = lax.top_k(logits, TOP_K)
    gates = jax.nn.softmax(top_v, axis=-1)
    dense_gate = jnp.sum(jax.nn.one_hot(top_i, N_EXPERTS, dtype=jnp.float32) * gates[..., None], axis=-2)
    y = jnp.zeros_like(x)
    for e in range(N_EXPERTS):
        y = y + dense_gate[..., e:e + 1].astype(x.dtype) * _swiglu(x, wg[e], wu[e], wd[e])
    return y


def _mixer(xn, p, l, slopes):
    B, L, _ = xn.shape
    z = xn @ p['w_in'][l]
    cuts = [QK_WIDTH, 2 * QK_WIDTH, 2 * QK_WIDTH + ATTN_WIDTH,
            2 * QK_WIDTH + ATTN_WIDTH + LRU_WIDTH, 2 * QK_WIDTH + ATTN_WIDTH + 2 * LRU_WIDTH,
            2 * QK_WIDTH + ATTN_WIDTH + 2 * LRU_WIDTH + D_MODEL]
    q, k, v, xl, gl, ga, gr = jnp.split(z, cuts, axis=-1)
    bga, bgr = jnp.split(p['b_gate'][l], 2, axis=-1)

    q = _rmsnorm(q.reshape(B, L, N_HEADS, 2, HEAD_DIM), p['q_norm'][l])
    k = _rmsnorm(k.reshape(B, L, N_HEADS, 2, HEAD_DIM), p['k_norm'][l])
    v = v.reshape(B, L, N_HEADS, V_DIM)
    lam_init = 0.8 - 0.6 * math.exp(-0.3 * l)
    f32 = jnp.float32
    lam = (jnp.exp(jnp.sum(p['lambda_q1'][l].astype(f32) * p['lambda_k1'][l].astype(f32)))
           - jnp.exp(jnp.sum(p['lambda_q2'][l].astype(f32) * p['lambda_k2'][l].astype(f32)))
           + lam_init)
    att = _diff_attention(q, k, v, slopes, lam)
    att = _rmsnorm(att, p['attn_subln'][l]) * (1.0 - lam_init)
    ya = att.reshape(B, L, ATTN_WIDTH) @ p['w_attn_branch'][l]

    xc = _centered_dwconv(xl, p['conv_w'][l], p['conv_b'][l])
    wa, ba, wx, bx, lm = p['lru_wa'][l], p['lru_ba'][l], p['lru_wx'][l], p['lru_bx'][l], p['lru_lambda'][l]
    h = (_rglru_dir(xc, wa[0], ba[0], wx[0], bx[0], lm[0], False)
         + _rglru_dir(xc, wa[1], ba[1], wx[1], bx[1], lm[1], True))
    yr = (h.astype(xn.dtype) * jax.nn.gelu(gl)) @ p['w_lru_branch'][l]

    merged = jax.nn.sigmoid(ga + bga) * ya + jax.nn.sigmoid(gr + bgr) * yr
    return merged @ p['w_out'][l]


def _trunk(x, p):
    B = x.shape[0]
    meta = jnp.broadcast_to(p['meta_tokens'][None].astype(x.dtype), (B, N_META, D_MODEL))
    h = jnp.concatenate([meta, x], axis=1)
    slopes = _alibi_slopes()
    for l in range(DEPTH):
        h = h + _mixer(_rmsnorm(h, p['norm_mix'][l]), p, l, slopes)
        hn = _rmsnorm(h, p['norm_ffn'][l])
        if l % 2 == 0:
            j = l // 2
            h = h + _swiglu(hn, p['ffn_w_gate'][j], p['ffn_w_up'][j], p['ffn_w_down'][j])
        else:
            j = l // 2
            h = h + _moe(hn, p['moe_router'][j], p['moe_w_gate'][j], p['moe_w_up'][j], p['moe_w_down'][j])
    return h[:, N_META:]


def setup_inputs(seed: int = 0) -> dict:
    key = jax.random.key(seed)
    ks = jax.random.split(key, 32)
    f32 = jnp.float32

    def nrm(k, shape, scale):
        return jax.random.normal(k, shape, f32) * scale

    def gain(k, shape):
        return 1.0 + 0.05 * jax.random.normal(k, shape, f32)

    u = jax.random.uniform(ks[20], (DEPTH, 2, LRU_WIDTH), f32, 0.9, 0.999)
    a0 = u ** (1.0 / LRU_C)
    lru_lambda = jnp.log(a0) - jnp.log1p(-a0)
    return {
        'x_prompt': nrm(ks[0], (BATCH, SEQ, D_MODEL), 1.0),
        'x_sample': nrm(ks[1], (DEC_BATCH, DEC_SEQ, D_MODEL), 1.0),
        'meta_tokens': nrm(ks[2], (N_META, D_MODEL), 1.0),
        'norm_mix': gain(ks[3], (DEPTH, D_MODEL)),
        'norm_ffn': gain(ks[4], (DEPTH, D_MODEL)),
        'w_in': nrm(ks[5], (DEPTH, D_MODEL, IN_WIDTH), D_MODEL ** -0.5),
        'b_gate': nrm(ks[6], (DEPTH, 2 * D_MODEL), 0.1),
        'q_norm': gain(ks[7], (DEPTH, HEAD_DIM)),
        'k_norm': gain(ks[8], (DEPTH, HEAD_DIM)),
        'lambda_q1': nrm(ks[9], (DEPTH, HEAD_DIM), 0.1),
        'lambda_k1': nrm(ks[10], (DEPTH, HEAD_DIM), 0.1),
        'lambda_q2': nrm(ks[11], (DEPTH, HEAD_DIM), 0.1),
        'lambda_k2': nrm(ks[12], (DEPTH, HEAD_DIM), 0.1),
        'attn_subln': gain(ks[13], (DEPTH, V_DIM)),
        'w_attn_branch': nrm(ks[14], (DEPTH, ATTN_WIDTH, D_MODEL), ATTN_WIDTH ** -0.5),
        'conv_w': nrm(ks[15], (DEPTH, CONV_W, LRU_WIDTH), CONV_W ** -0.5),
        'conv_b': nrm(ks[16], (DEPTH, LRU_WIDTH), 0.01),
        'lru_wa': nrm(ks[17], (DEPTH, 2, LRU_BLOCKS, LRU_BLOCK_W, LRU_BLOCK_W), LRU_BLOCK_W ** -0.5),
        'lru_ba': nrm(ks[18], (DEPTH, 2, LRU_WIDTH), 0.1),
        'lru_wx': nrm(ks[19], (DEPTH, 2, LRU_BLOCKS, LRU_BLOCK_W, LRU_BLOCK_W), LRU_BLOCK_W ** -0.5),
        'lru_bx': nrm(ks[21], (DEPTH, 2, LRU_WIDTH), 0.1),
        'lru_lambda': lru_lambda,
        'w_lru_branch': nrm(ks[22], (DEPTH, LRU_WIDTH, D_MODEL), LRU_WIDTH ** -0.5),
        'w_out': nrm(ks[23], (DEPTH, D_MODEL, D_MODEL), D_MODEL ** -0.5),
        'ffn_w_gate': nrm(ks[24], (N_DENSE, D_MODEL, D_FF), D_MODEL ** -0.5),
        'ffn_w_up': nrm(ks[25], (N_DENSE, D_MODEL, D_FF), D_MODEL ** -0.5),
        'ffn_w_down': nrm(ks[26], (N_DENSE, D_FF, D_MODEL), D_FF ** -0.5),
        'moe_router': nrm(ks[27], (N_MOE, D_MODEL, N_EXPERTS), D_MODEL ** -0.5),
        'moe_w_gate': nrm(ks[28], (N_MOE, N_EXPERTS, D_MODEL, D_FF_EXPERT), D_MODEL ** -0.5),
        'moe_w_up': nrm(ks[29], (N_MOE, N_EXPERTS, D_MODEL, D_FF_EXPERT), D_MODEL ** -0.5),
        'moe_w_down': nrm(ks[30], (N_MOE, N_EXPERTS, D_FF_EXPERT, D_MODEL), D_FF_EXPERT ** -0.5),
    }


def reference(x_prompt, x_sample, meta_tokens, norm_mix, norm_ffn, w_in, b_gate, q_norm, k_norm,
              lambda_q1, lambda_k1, lambda_q2, lambda_k2, attn_subln, w_attn_branch,
              conv_w, conv_b, lru_wa, lru_ba, lru_wx, lru_bx, lru_lambda, w_lru_branch, w_out,
              ffn_w_gate, ffn_w_up, ffn_w_down, moe_router, moe_w_gate, moe_w_up, moe_w_down):
    p = dict(meta_tokens=meta_tokens, norm_mix=norm_mix, norm_ffn=norm_ffn, w_in=w_in,
             b_gate=b_gate, q_norm=q_norm, k_norm=k_norm, lambda_q1=lambda_q1,
             lambda_k1=lambda_k1, lambda_q2=lambda_q2, lambda_k2=lambda_k2,
             attn_subln=attn_subln, w_attn_branch=w_attn_branch, conv_w=conv_w, conv_b=conv_b,
             lru_wa=lru_wa, lru_ba=lru_ba, lru_wx=lru_wx, lru_bx=lru_bx, lru_lambda=lru_lambda,
             w_lru_branch=w_lru_branch, w_out=w_out, ffn_w_gate=ffn_w_gate, ffn_w_up=ffn_w_up,
             ffn_w_down=ffn_w_down, moe_router=moe_router, moe_w_gate=moe_w_gate,
             moe_w_up=moe_w_up, moe_w_down=moe_w_down)
    y_prompt = _trunk(x_prompt, p)
    y_sample = _trunk(x_sample, p)
    return (y_prompt, y_sample)
```

```python
import functools
import math

import jax
import jax.numpy as jnp
from jax import lax
from jax.experimental import pallas as pl
from jax.experimental.pallas import tpu as pltpu

F32 = jnp.float32
BF16 = jnp.bfloat16

D_MODEL = 1024
N_META = 16
N_HEADS = 8
HEAD_DIM = 64
V_DIM = 2 * HEAD_DIM
LRU_BLOCKS = 16
LRU_BLOCK_W = D_MODEL // LRU_BLOCKS
LRU_C = 8.0
N_EXPERTS = 8
EPS = 1e-6
IN_WIDTH = 7 * D_MODEL

LANE = 128
VMEM_LIMIT_BYTES = 48 * 1024 * 1024
NEG_BIG = -1e30

LRU_CB = 128
LRU_TT = 128


def _largest_tile(n, target, mult=LANE):
    assert n % mult == 0, (n, mult)
    best = mult
    t = mult
    while t <= min(n, target):
        if n % t == 0:
            best = t
        t += mult
    return best


def _cparams(*sem):
    return pltpu.CompilerParams(dimension_semantics=sem, vmem_limit_bytes=VMEM_LIMIT_BYTES)


def _rmsnorm_rows(x, g):
    ms = jnp.mean(x * x, axis=-1, keepdims=True)
    return x * lax.rsqrt(ms + EPS) * g


def _in_proj_kernel(h_ref, g_ref, w_ref, z_ref, xn_ref):
    @pl.when(pl.program_id(1) == 0)
    def _():
        xn_ref[...] = _rmsnorm_rows(h_ref[...], g_ref[...]).astype(BF16)

    z_ref[...] = jnp.dot(xn_ref[...], w_ref[...], preferred_element_type=F32)


def _in_proj(h2, g, w):
    m = h2.shape[0]
    tm = _largest_tile(m, 1024)
    tn = 512
    return pl.pallas_call(
        _in_proj_kernel,
        grid=(m // tm, IN_WIDTH // tn),
        in_specs=[
            pl.BlockSpec((tm, D_MODEL), lambda i, j: (i, 0)),
            pl.BlockSpec((1, D_MODEL), lambda i, j: (0, 0)),
            pl.BlockSpec((D_MODEL, tn), lambda i, j: (0, j)),
        ],
        out_specs=pl.BlockSpec((tm, tn), lambda i, j: (i, j)),
        out_shape=jax.ShapeDtypeStruct((m, IN_WIDTH), F32),
        scratch_shapes=[pltpu.VMEM((tm, D_MODEL), BF16)],
        compiler_params=_cparams("parallel", "arbitrary"),
        name="in_proj",
    )(h2, g, w)


def _attn_kernel(sc_ref, q_ref, k_ref, v_ref, qg_ref, kg_ref, lam_ref, sg_ref, o_ref,
                 kb_ref, vb_ref, m_ref, l_ref, acc_ref, *, seq_len, tq, tk):
    lp = k_ref.shape[0]
    nk = lp // tk
    head = pl.program_id(1)
    qi = pl.program_id(2)
    lane = lax.broadcasted_iota(jnp.int32, (1, LANE), 1)
    lo = lane < HEAD_DIM

    def half_rmsnorm(x, g):
        sq = x * x
        s_lo = jnp.sum(jnp.where(lo, sq, 0.0), axis=-1, keepdims=True)
        s_hi = jnp.sum(jnp.where(lo, 0.0, sq), axis=-1, keepdims=True)
        ms = jnp.where(lo, s_lo, s_hi) * (1.0 / HEAD_DIM)
        return x * lax.rsqrt(ms + EPS) * g

    @pl.when(qi == 0)
    def _():
        def fill(c, carry):
            r = pl.ds(pl.multiple_of(c * tk, tk), tk)
            kb_ref[r, :] = half_rmsnorm(k_ref[r, :], kg_ref[...]).astype(BF16)
            vb_ref[r, :] = v_ref[r, :].astype(BF16)
            return carry
        lax.fori_loop(0, nk, fill, 0)

    qn = half_rmsnorm(q_ref[...], qg_ref[...]) * (1.0 / math.sqrt(HEAD_DIM))
    q_comp = (jnp.where(lo, qn, 0.0).astype(BF16), jnp.where(lo, 0.0, qn).astype(BF16))

    m_ref[...] = jnp.full(m_ref.shape, NEG_BIG, F32)
    l_ref[...] = jnp.zeros(l_ref.shape, F32)
    acc_ref[...] = jnp.zeros(acc_ref.shape, F32)

    slope = sc_ref[head]
    rel = (lax.broadcasted_iota(jnp.int32, (tq, tk), 0)
           - lax.broadcasted_iota(jnp.int32, (tq, tk), 1)).astype(F32)
    col = lax.broadcasted_iota(jnp.int32, (1, tk), 1)

    def chunk(kc, carry):
        k0 = pl.multiple_of(kc * tk, tk)
        kblk = kb_ref[pl.ds(k0, tk), :]
        vblk = vb_ref[pl.ds(k0, tk), :]
        off = (qi * tq - k0).astype(F32)
        bias = -slope * jnp.abs(rel + off) + jnp.where(col + k0 < seq_len, 0.0, NEG_BIG)
        for c in range(2):
            s = lax.dot_general(q_comp[c], kblk, (((1,), (1,)), ((), ())),
                                preferred_element_type=F32) + bias
            m_prev = m_ref[c]
            m_new = jnp.maximum(m_prev, jnp.max(s, axis=-1, keepdims=True))
            alpha = jnp.exp(m_prev - m_new)
            p = jnp.exp(s - m_new)
            l_ref[c] = alpha * l_ref[c] + jnp.sum(p, axis=-1, keepdims=True)
            acc_ref[c] = alpha * acc_ref[c] + jnp.dot(p.astype(BF16), vblk,
                                                      preferred_element_type=F32)
            m_ref[c] = m_new
        return carry

    lax.fori_loop(0, nk, chunk, 0)

    lam_init = sc_ref[N_HEADS]
    lp_ = lam_ref[...]
    lam = (jnp.exp(jnp.sum(lp_[0:1] * lp_[1:2], axis=-1, keepdims=True))
           - jnp.exp(jnp.sum(lp_[2:3] * lp_[3:4], axis=-1, keepdims=True)) + lam_init)
    o = acc_ref[0] / l_ref[0] - lam * (acc_ref[1] / l_ref[1])
    o = _rmsnorm_rows(o, sg_ref[...]) * (1.0 - lam_init)
    o_ref[...] = o.astype(BF16)


def _attention(z3, scal, qg, kg, lamp, sg, seq_len):
    b, lp, _ = z3.shape
    tq = _largest_tile(lp, 768)
    tk = tq
    kern = functools.partial(_attn_kernel, seq_len=seq_len, tq=tq, tk=tk)
    small = lambda shape: pl.BlockSpec(shape, lambda bi, hi, qi: (0, 0))
    return pl.pallas_call(
        kern,
        grid=(b, N_HEADS, lp // tq),
        in_specs=[
            pl.BlockSpec(memory_space=pltpu.SMEM),
            pl.BlockSpec((None, tq, V_DIM), lambda bi, hi, qi: (bi, qi, hi)),
            pl.BlockSpec((None, lp, V_DIM), lambda bi, hi, qi: (bi, 0, N_HEADS + hi)),
            pl.BlockSpec((None, lp, V_DIM), lambda bi, hi, qi: (bi, 0, 2 * N_HEADS + hi)),
            small((1, V_DIM)), small((1, V_DIM)), small((4, HEAD_DIM)), small((1, V_DIM)),
        ],
        out_specs=pl.BlockSpec((None, tq, V_DIM), lambda bi, hi, qi: (bi, qi, hi)),
        out_shape=jax.ShapeDtypeStruct((b, lp, N_HEADS * V_DIM), BF16),
        scratch_shapes=[
            pltpu.VMEM((lp, V_DIM), BF16), pltpu.VMEM((lp, V_DIM), BF16),
            pltpu.VMEM((2, tq, 1), F32), pltpu.VMEM((2, tq, 1), F32),
            pltpu.VMEM((2, tq, V_DIM), F32),
        ],
        compiler_params=_cparams("parallel", "parallel", "arbitrary"),
        name="diff_attention",
    )(scal, z3, z3, z3, qg, kg, lamp, sg)


def _lru_kernel(xl_ref, gl_ref, cw_ref, cbias_ref, w_ref, ba_ref, bx_ref, lam_ref, o_ref,
                xp_ref, hf_ref, *, seq_len, tt):
    lp, cb = xl_ref.shape
    n = lp // tt
    pad = 8
    rows_t = lax.broadcasted_iota(jnp.int32, (tt, 1), 0)

    xp_ref[0:pad, :] = jnp.zeros((pad, cb), F32)
    xp_ref[lp + pad:lp + 2 * pad, :] = jnp.zeros((pad, cb), F32)

    def copy_in(c, carry):
        r0 = pl.multiple_of(c * tt, tt)
        x = xl_ref[pl.ds(r0, tt), :]
        xp_ref[pl.ds(r0 + pad, tt), :] = jnp.where(rows_t + r0 < seq_len, x, 0.0)
        return carry
    lax.fori_loop(0, n, copy_in, 0)

    cw = cw_ref[...]

    def gates(c, d):
        r0 = pl.multiple_of(c * tt, tt)
        win = xp_ref[pl.ds(r0, tt + 2 * pad), :]
        wn = tt + 2 * pad
        xc = (pltpu.roll(win, 2, 0)[pad:pad + tt] * cw[0:1]
              + pltpu.roll(win, 1, 0)[pad:pad + tt] * cw[1:2]
              + win[pad:pad + tt] * cw[2:3]
              + pltpu.roll(win, wn - 1, 0)[pad:pad + tt] * cw[3:4]
              + cbias_ref[...])
        pre = jnp.dot(xc.astype(BF16), w_ref[:, d * 2 * cb:(d + 1) * 2 * cb],
                      preferred_element_type=F32)
        r = jax.nn.sigmoid(pre[:, :cb] + ba_ref[d:d + 1, :])
        i = jax.nn.sigmoid(pre[:, cb:] + bx_ref[d:d + 1, :])
        nl = -lam_ref[d:d + 1, :]
        softplus = jnp.maximum(nl, 0.0) + jnp.log1p(jnp.exp(-jnp.abs(nl)))
        log_a = (-LRU_C * r) * softplus
        a = jnp.exp(log_a)
        th = jnp.tanh(log_a)
        mult = jnp.sqrt(-2.0 * th / (1.0 - th))
        u = mult * (i * xc)
        u = jnp.where(rows_t + r0 < seq_len, u, 0.0)
        return a, u

    def scan_chunk(a, u, reverse):
        d = 1
        while d < tt:
            if reverse:
                shift, keep = tt - d, rows_t < tt - d
            else:
                shift, keep = d, rows_t >= d
            u = u + a * jnp.where(keep, pltpu.roll(u, shift, 0), 0.0)
            a = a * jnp.where(keep, pltpu.roll(a, shift, 0), 1.0)
            d *= 2
        return a, u

    def fwd(c, carry):
        a, u = gates(c, 0)
        a, u = scan_chunk(a, u, False)
        h = u + a * carry
        hf_ref[pl.ds(pl.multiple_of(c * tt, tt), tt), :] = h
        return h[tt - 1:tt, :]
    lax.fori_loop(0, n, fwd, jnp.zeros((1, cb), F32))

    def bwd(j, carry):
        c = n - 1 - j
        r = pl.ds(pl.multiple_of(c * tt, tt), tt)
        a, u = gates(c, 1)
        a, u = scan_chunk(a, u, True)
        h = u + a * carry
        o_ref[r, :] = ((hf_ref[r, :] + h) * jax.nn.gelu(gl_ref[r, :])).astype(BF16)
        return h[0:1, :]
    lax.fori_loop(0, n, bwd, jnp.zeros((1, cb), F32))


def _lru(z3, cw, cbias, wcat, ba, bx, lam, seq_len):
    b, lp, _ = z3.shape
    cb = LRU_CB
    nc = D_MODEL // cb
    kern = functools.partial(_lru_kernel, seq_len=seq_len, tt=LRU_TT)
    chan = lambda rows: pl.BlockSpec((rows, cb), lambda bi, ci: (0, ci))
    return pl.pallas_call(
        kern,
        grid=(b, nc),
        in_specs=[
            pl.BlockSpec((None, lp, cb), lambda bi, ci: (bi, 0, 3 * nc + ci)),
            pl.BlockSpec((None, lp, cb), lambda bi, ci: (bi, 0, 4 * nc + ci)),
            chan(4), chan(1),
            pl.BlockSpec((None, cb, 4 * cb), lambda bi, ci: (ci, 0, 0)),
            chan(2), chan(2), chan(2),
        ],
        out_specs=pl.BlockSpec((None, lp, cb), lambda bi, ci: (bi, 0, ci)),
        out_shape=jax.ShapeDtypeStruct((b, lp, D_MODEL), BF16),
        scratch_shapes=[pltpu.VMEM((lp + 16, cb), F32), pltpu.VMEM((lp, cb), F32)],
        compiler_params=_cparams("parallel", "parallel"),
        name="rglru",
    )(z3, z3, cw, cbias, wcat, ba, bx, lam)


def _merge_kernel(att_ref, yl_ref, ga_ref, gr_ref, h_ref, wa_ref, wl_ref, wo_ref,
                  bga_ref, bgr_ref, o_ref):
    ya = jnp.dot(att_ref[...], wa_ref[...], preferred_element_type=F32)
    yr = jnp.dot(yl_ref[...], wl_ref[...], preferred_element_type=F32)
    merged = (jax.nn.sigmoid(ga_ref[...] + bga_ref[...]) * ya
              + jax.nn.sigmoid(gr_ref[...] + bgr_ref[...]) * yr)
    o_ref[...] = h_ref[...] + jnp.dot(merged.astype(BF16), wo_ref[...],
                                      preferred_element_type=F32)


def _merge(att2, yl2, z2, h2, wa, wl, wo, bga, bgr):
    m = h2.shape[0]
    tm = _largest_tile(m, 256)
    row = lambda col: pl.BlockSpec((tm, D_MODEL), lambda i: (i, col))
    full = lambda shape: pl.BlockSpec(shape, lambda i: (0, 0))
    return pl.pallas_call(
        _merge_kernel,
        grid=(m // tm,),
        in_specs=[row(0), row(0), row(5), row(6), row(0),
                  full((D_MODEL, D_MODEL)), full((D_MODEL, D_MODEL)), full((D_MODEL, D_MODEL)),
                  full((1, D_MODEL)), full((1, D_MODEL))],
        out_specs=row(0),
        out_shape=jax.ShapeDtypeStruct((m, D_MODEL), F32),
        compiler_params=_cparams("parallel"),
        name="merge_out_proj",
    )(att2, yl2, z2, z2, h2, wa, wl, wo, bga, bgr)


def _ffn_kernel(*refs, chunks_per_expert):
    if chunks_per_expert:
        h_ref, g_ref, wr_ref, wg_ref, wu_ref, wd_ref, o_ref, hn_ref, acc_ref, gate_ref = refs
    else:
        h_ref, g_ref, wg_ref, wu_ref, wd_ref, o_ref, hn_ref, acc_ref = refs
    j = pl.program_id(1)

    @pl.when(j == 0)
    def _():
        h = h_ref[...]
        hn = _rmsnorm_rows(h, g_ref[...])
        hn_ref[...] = hn.astype(BF16)
        acc_ref[...] = h
        if chunks_per_expert:
            logits = jnp.dot(hn, wr_ref[...], preferred_element_type=F32,
                             precision=lax.Precision.HIGHEST)
            lane = lax.broadcasted_iota(jnp.int32, logits.shape, 1).astype(F32)
            lg = jnp.where(lane < N_EXPERTS, logits, NEG_BIG)
            m1 = jnp.max(lg, axis=-1, keepdims=True)
            i1 = jnp.min(jnp.where(lg == m1, lane, float(LANE)), axis=-1, keepdims=True)
            lg2 = jnp.where(lane == i1, NEG_BIG, lg)
            m2 = jnp.max(lg2, axis=-1, keepdims=True)
            i2 = jnp.min(jnp.where(lg2 == m2, lane, float(LANE)), axis=-1, keepdims=True)
            e2 = jnp.exp(m2 - m1)
            den = 1.0 + e2
            gate_ref[...] = jnp.where(lane == i1, 1.0 / den,
                                      jnp.where(lane == i2, e2 / den, 0.0))

    hn = hn_ref[...]
    gg = jnp.dot(hn, wg_ref[...], preferred_element_type=F32)
    uu = jnp.dot(hn, wu_ref[...], preferred_element_type=F32)
    act = (jax.nn.silu(gg) * uu).astype(BF16)
    y = jnp.dot(act, wd_ref[...], preferred_element_type=F32)
    if chunks_per_expert:
        e = (j // chunks_per_expert).astype(F32)
        gates = gate_ref[...]
        lane = lax.broadcasted_iota(jnp.int32, gates.shape, 1).astype(F32)
        y = y * jnp.sum(jnp.where(lane == e, gates, 0.0), axis=-1, keepdims=True)
    acc_ref[...] += y

    @pl.when(j == pl.num_programs(1) - 1)
    def _():
        o_ref[...] = acc_ref[...]


def _ffn(h2, g, wg, wu, wd, router=None, d_ff_expert=None):
    m = h2.shape[0]
    f = wg.shape[1]
    tm = _largest_tile(m, 512)
    tf = 512
    in_specs = [pl.BlockSpec((tm, D_MODEL), lambda i, j: (i, 0)),
                pl.BlockSpec((1, D_MODEL), lambda i, j: (0, 0))]
    args = [h2, g]
    scratch = [pltpu.VMEM((tm, D_MODEL), BF16), pltpu.VMEM((tm, D_MODEL), F32)]
    cpe = 0
    if router is not None:
        assert d_ff_expert % tf == 0
        cpe = d_ff_expert // tf
        in_specs.append(pl.BlockSpec((D_MODEL, LANE), lambda i, j: (0, 0)))
        args.append(router)
        scratch.append(pltpu.VMEM((tm, LANE), F32))
    in_specs += [pl.BlockSpec((D_MODEL, tf), lambda i, j: (0, j)),
                 pl.BlockSpec((D_MODEL, tf), lambda i, j: (0, j)),
                 pl.BlockSpec((tf, D_MODEL), lambda i, j: (j, 0))]
    args += [wg, wu, wd]
    return pl.pallas_call(
        functools.partial(_ffn_kernel, chunks_per_expert=cpe),
        grid=(m // tm, f // tf),
        in_specs=in_specs,
        out_specs=pl.BlockSpec((tm, D_MODEL), lambda i, j: (i, 0)),
        out_shape=jax.ShapeDtypeStruct((m, D_MODEL), F32),
        scratch_shapes=scratch,
        compiler_params=_cparams("parallel", "arbitrary"),
        name="moe_ffn" if cpe else "dense_ffn",
    )(*args)


def _blockdiag_groups(w, per):
    g = LRU_BLOCKS // per
    w = w.reshape(g, per, LRU_BLOCK_W, LRU_BLOCK_W)
    eye = jnp.eye(per, dtype=w.dtype)
    out = jnp.einsum('gpio,pq->gpiqo', w, eye)
    return out.reshape(g, per * LRU_BLOCK_W, per * LRU_BLOCK_W)


def _prepare(p):
    depth = p['w_in'].shape[0]
    per = LRU_CB // LRU_BLOCK_W
    layers = []
    slopes = jnp.exp2(-8.0 * jnp.arange(1, N_HEADS + 1, dtype=F32) / N_HEADS)
    for l in range(depth):
        lam_init = 0.8 - 0.6 * math.exp(-0.3 * l)
        wcat = jnp.concatenate(
            [_blockdiag_groups(p['lru_wa'][l, 0], per), _blockdiag_groups(p['lru_wx'][l, 0], per),
             _blockdiag_groups(p['lru_wa'][l, 1], per), _blockdiag_groups(p['lru_wx'][l, 1], per)],
            axis=-1).astype(BF16)
        lay = dict(
            norm_mix=p['norm_mix'][l][None], norm_ffn=p['norm_ffn'][l][None],
            w_in=p['w_in'][l].astype(BF16),
            scal=jnp.concatenate([slopes, jnp.full((N_HEADS,), lam_init, F32)]),
            qg=jnp.tile(p['q_norm'][l], 2)[None], kg=jnp.tile(p['k_norm'][l], 2)[None],
            lamp=jnp.stack([p['lambda_q1'][l], p['lambda_k1'][l],
                            p['lambda_q2'][l], p['lambda_k2'][l]]),
            sg=p['attn_subln'][l][None],
            conv_w=p['conv_w'][l], conv_b=p['conv_b'][l][None], wcat=wcat,
            ba=p['lru_ba'][l], bx=p['lru_bx'][l], lam=p['lru_lambda'][l],
            wa=p['w_attn_branch'][l].astype(BF16), wl=p['w_lru_branch'][l].astype(BF16),
            wo=p['w_out'][l].astype(BF16),
            bga=p['b_gate'][l, :D_MODEL][None], bgr=p['b_gate'][l, D_MODEL:][None],
        )
        j = l // 2
        if l % 2 == 0:
            lay.update(wg=p['ffn_w_gate'][j].astype(BF16), wu=p['ffn_w_up'][j].astype(BF16),
                       wd=p['ffn_w_down'][j].astype(BF16))
        else:
            ne, _, fe = p['moe_w_gate'][j].shape
            lay.update(
                wg=jnp.moveaxis(p['moe_w_gate'][j], 0, 1).reshape(D_MODEL, ne * fe).astype(BF16),
                wu=jnp.moveaxis(p['moe_w_up'][j], 0, 1).reshape(D_MODEL, ne * fe).astype(BF16),
                wd=p['moe_w_down'][j].reshape(ne * fe, D_MODEL).astype(BF16),
                router=jnp.pad(p['moe_router'][j], ((0, 0), (0, LANE - ne))),
                d_ff_expert=fe)
        layers.append(lay)
    return layers


def _trunk(x, meta, layers):
    b, s, _ = x.shape
    seq_len = s + N_META
    lp = -(-seq_len // LANE) * LANE
    h = jnp.concatenate(
        [jnp.broadcast_to(meta[None].astype(x.dtype), (b, N_META, D_MODEL)), x,
         jnp.zeros((b, lp - seq_len, D_MODEL), x.dtype)], axis=1)
    h2 = h.reshape(b * lp, D_MODEL)
    for lay in layers:
        z2 = _in_proj(h2, lay['norm_mix'], lay['w_in'])
        z3 = z2.reshape(b, lp, IN_WIDTH)
        att = _attention(z3, lay['scal'], lay['qg'], lay['kg'], lay['lamp'], lay['sg'], seq_len)
        yl = _lru(z3, lay['conv_w'], lay['conv_b'], lay['wcat'], lay['ba'], lay['bx'],
                  lay['lam'], seq_len)
        h2 = _merge(att.reshape(b * lp, D_MODEL), yl.reshape(b * lp, D_MODEL), z2, h2,
                    lay['wa'], lay['wl'], lay['wo'], lay['bga'], lay['bgr'])
        h2 = _ffn(h2, lay['norm_ffn'], lay['wg'], lay['wu'], lay['wd'],
                  lay.get('router'), lay.get('d_ff_expert'))
    return h2.reshape(b, lp, D_MODEL)[:, N_META:seq_len]


def kernel(x_prompt, x_sample, meta_tokens, norm_mix, norm_ffn, w_in, b_gate, q_norm, k_norm, lambda_q1, lambda_k1, lambda_q2, lambda_k2, attn_subln, w_attn_branch, conv_w, conv_b, lru_wa, lru_ba, lru_wx, lru_bx, lru_lambda, w_lru_branch, w_out, ffn_w_gate, ffn_w_up, ffn_w_down, moe_router, moe_w_gate, moe_w_up, moe_w_down):
    p = dict(norm_mix=norm_mix, norm_ffn=norm_ffn, w_in=w_in, b_gate=b_gate, q_norm=q_norm,
             k_norm=k_norm, lambda_q1=lambda_q1, lambda_k1=lambda_k1, lambda_q2=lambda_q2,
             lambda_k2=lambda_k2, attn_subln=attn_subln, w_attn_branch=w_attn_branch,
             conv_w=conv_w, conv_b=conv_b, lru_wa=lru_wa, lru_ba=lru_ba, lru_wx=lru_wx,
             lru_bx=lru_bx, lru_lambda=lru_lambda, w_lru_branch=w_lru_branch, w_out=w_out,
             ffn_w_gate=ffn_w_gate, ffn_w_up=ffn_w_up, ffn_w_down=ffn_w_down,
             moe_router=moe_router, moe_w_gate=moe_w_gate, moe_w_up=moe_w_up,
             moe_w_down=moe_w_down)
    layers = _prepare(p)
    return (_trunk(x_prompt, meta_tokens, layers), _trunk(x_sample, meta_tokens, layers))
```

```python
import functools
import math

import jax
import jax.numpy as jnp
from jax import lax
from jax.experimental import pallas as pl
from jax.experimental.pallas import tpu as pltpu

F32 = jnp.float32
BF16 = jnp.bfloat16

D_MODEL = 1024
N_META = 16
N_HEADS = 8
HEAD_DIM = 64
V_DIM = 2 * HEAD_DIM
LRU_BLOCKS = 16
LRU_BLOCK_W = D_MODEL // LRU_BLOCKS
LRU_C = 8.0
N_EXPERTS = 8
EPS = 1e-6
IN_WIDTH = 7 * D_MODEL

LANE = 128
SUBLANE = 8
VMEM_LIMIT_BYTES = 48 * 1024 * 1024
NEG_BIG = -1e30

ATT_TILE = 512
ATT_UNROLL = 8
ATT_TAIL = LANE
ATT_VMEM_LIMIT_BYTES = 56 * 1024 * 1024
MASK_LOGIT = -30000.0
SAFE_LOGIT_SPAN = 80.0

LRU_CB = 128
LRU_TT = 128


def _largest_tile(n, target, mult=LANE):
    assert n % mult == 0, (n, mult)
    best = mult
    t = mult
    while t <= min(n, target):
        if n % t == 0:
            best = t
        t += mult
    return best


def _cparams(*sem):
    return pltpu.CompilerParams(dimension_semantics=sem, vmem_limit_bytes=VMEM_LIMIT_BYTES)


def _rmsnorm_rows(x, g):
    ms = jnp.mean(x * x, axis=-1, keepdims=True)
    return x * lax.rsqrt(ms + EPS) * g


def _in_proj_kernel(h_ref, g_ref, w_ref, z_ref, xn_ref):
    @pl.when(pl.program_id(1) == 0)
    def _():
        xn_ref[...] = _rmsnorm_rows(h_ref[...], g_ref[...]).astype(BF16)

    z_ref[...] = jnp.dot(xn_ref[...], w_ref[...], preferred_element_type=F32)


def _in_proj(h2, g, w):
    m = h2.shape[0]
    tm = _largest_tile(m, 1024)
    tn = 512
    return pl.pallas_call(
        _in_proj_kernel,
        grid=(m // tm, IN_WIDTH // tn),
        in_specs=[
            pl.BlockSpec((tm, D_MODEL), lambda i, j: (i, 0)),
            pl.BlockSpec((1, D_MODEL), lambda i, j: (0, 0)),
            pl.BlockSpec((D_MODEL, tn), lambda i, j: (0, j)),
        ],
        out_specs=pl.BlockSpec((tm, tn), lambda i, j: (i, j)),
        out_shape=jax.ShapeDtypeStruct((m, IN_WIDTH), F32),
        scratch_shapes=[pltpu.VMEM((tm, D_MODEL), BF16)],
        compiler_params=_cparams("parallel", "arbitrary"),
        name="in_proj",
    )(h2, g, w)


def _attn_kernel(sc_ref, q_ref, k_ref, v_ref, qg_ref, kg_ref, lam_ref, sg_ref, o_ref,
                 kb_ref, vb_ref, m_ref, l_ref, acc_ref, *, seq_len, tq, tk):
    lp = k_ref.shape[0]
    nk = lp // tk
    head = pl.program_id(1)
    qi = pl.program_id(2)
    lane = lax.broadcasted_iota(jnp.int32, (1, LANE), 1)
    lo = lane < HEAD_DIM

    def half_rmsnorm(x, g):
        sq = x * x
        s_lo = jnp.sum(jnp.where(lo, sq, 0.0), axis=-1, keepdims=True)
        s_hi = jnp.sum(jnp.where(lo, 0.0, sq), axis=-1, keepdims=True)
        ms = jnp.where(lo, s_lo, s_hi) * (1.0 / HEAD_DIM)
        return x * lax.rsqrt(ms + EPS) * g

    @pl.when(qi == 0)
    def _():
        def fill(c, carry):
            r = pl.ds(pl.multiple_of(c * tk, tk), tk)
            kb_ref[r, :] = half_rmsnorm(k_ref[r, :], kg_ref[...]).astype(BF16)
            vb_ref[r, :] = v_ref[r, :].astype(BF16)
            return carry
        lax.fori_loop(0, nk, fill, 0)

    qn = half_rmsnorm(q_ref[...], qg_ref[...]) * (1.0 / math.sqrt(HEAD_DIM))
    q_comp = (jnp.where(lo, qn, 0.0).astype(BF16), jnp.where(lo, 0.0, qn).astype(BF16))

    m_ref[...] = jnp.full(m_ref.shape, NEG_BIG, F32)
    l_ref[...] = jnp.zeros(l_ref.shape, F32)
    acc_ref[...] = jnp.zeros(acc_ref.shape, F32)

    slope = sc_ref[head]
    rel = (lax.broadcasted_iota(jnp.int32, (tq, tk), 0)
           - lax.broadcasted_iota(jnp.int32, (tq, tk), 1)).astype(F32)
    col = lax.broadcasted_iota(jnp.int32, (1, tk), 1)

    def chunk(kc, carry):
        k0 = pl.multiple_of(kc * tk, tk)
        kblk = kb_ref[pl.ds(k0, tk), :]
        vblk = vb_ref[pl.ds(k0, tk), :]
        off = (qi * tq - k0).astype(F32)
        bias = -slope * jnp.abs(rel + off) + jnp.where(col + k0 < seq_len, 0.0, NEG_BIG)
        for c in range(2):
            s = lax.dot_general(q_comp[c], kblk, (((1,), (1,)), ((), ())),
                                preferred_element_type=F32) + bias
            m_prev = m_ref[c]
            m_new = jnp.maximum(m_prev, jnp.max(s, axis=-1, keepdims=True))
            alpha = jnp.exp(m_prev - m_new)
            p = jnp.exp(s - m_new)
            l_ref[c] = alpha * l_ref[c] + jnp.sum(p, axis=-1, keepdims=True)
            acc_ref[c] = alpha * acc_ref[c] + jnp.dot(p.astype(BF16), vblk,
                                                      preferred_element_type=F32)
            m_ref[c] = m_new
        return carry

    lax.fori_loop(0, nk, chunk, 0)

    lam_init = sc_ref[N_HEADS]
    lp_ = lam_ref[...]
    lam = (jnp.exp(jnp.sum(lp_[0:1] * lp_[1:2], axis=-1, keepdims=True))
           - jnp.exp(jnp.sum(lp_[2:3] * lp_[3:4], axis=-1, keepdims=True)) + lam_init)
    o = acc_ref[0] / l_ref[0] - lam * (acc_ref[1] / l_ref[1])
    o = _rmsnorm_rows(o, sg_ref[...]) * (1.0 - lam_init)
    o_ref[...] = o.astype(BF16)


def _attn_bounded_kernel(sc_ref, q_ref, k_ref, v_ref, qg_ref, kg_ref, lam_ref, sg_ref, o_ref,
                         kq_ref, vq_ref, lhs_ref, bias_ref, acc_ref, *, seq_len, t):
    lp = k_ref.shape[0]
    s_main = lp - ATT_TAIL
    n_main = s_main // t
    slope = sc_ref[pl.program_id(1)]
    lam_init = sc_ref[N_HEADS]
    lane = lax.broadcasted_iota(jnp.int32, (1, LANE), 1)
    lo = lane < HEAD_DIM
    nt = (((1,), (1,)), ((), ()))

    def half_rmsnorm(x, g):
        sq = x * x
        s_lo = jnp.sum(jnp.where(lo, sq, 0.0), axis=-1, keepdims=True)
        s_hi = jnp.sum(jnp.where(lo, 0.0, sq), axis=-1, keepdims=True)
        ms = jnp.where(lo, s_lo, s_hi) * (1.0 / HEAD_DIM)
        return x * lax.rsqrt(ms + EPS) * g

    def split2(x):
        hi = x.astype(BF16).astype(F32)
        return hi, x - hi

    def pos_col(r0, n):
        return lax.broadcasted_iota(jnp.int32, (n, 1), 0) + r0

    lp_ = lam_ref[...]
    lam = (jnp.exp(jnp.sum(lp_[0:1] * lp_[1:2], axis=-1, keepdims=True))
           - jnp.exp(jnp.sum(lp_[2:3] * lp_[3:4], axis=-1, keepdims=True)) + lam_init)

    def fill(r0, n, kmax):
        rows = pos_col(r0, n)
        valid = rows < seq_len
        kn = jnp.where(valid, half_rmsnorm(k_ref[pl.ds(r0, n), :], kg_ref[...]), 0.0)
        sq = kn * kn
        n_lo = jnp.max(jnp.sum(jnp.where(lo, sq, 0.0), axis=-1, keepdims=True), axis=0, keepdims=True)
        n_hi = jnp.max(jnp.sum(jnp.where(lo, 0.0, sq), axis=-1, keepdims=True), axis=0, keepdims=True)
        sj_hi, sj_lo = split2(slope * rows.astype(F32))
        aug = jnp.where(lane == 0, -1.0,
              jnp.where((lane == 1) | (lane == 2), 1.0,
              jnp.where(lane == 3, sj_hi,
              jnp.where(lane == 4, sj_lo,
              jnp.where(lane == 5, jnp.where(valid, 0.0, MASK_LOGIT), 0.0)))))
        kq_ref[pl.ds(r0, n), 0:LANE] = kn.astype(BF16)
        kq_ref[pl.ds(r0, n), LANE:2 * LANE] = aug.astype(BF16)
        vq_ref[pl.ds(r0, n), 0:LANE] = jnp.where(valid, v_ref[pl.ds(r0, n), :], 0.0).astype(BF16)
        vq_ref[pl.ds(r0, n), LANE:2 * LANE] = jnp.broadcast_to(
            jnp.where(lane == 0, 1.0, 0.0), (n, LANE)).astype(BF16)
        return jnp.maximum(kmax[0], n_lo), jnp.maximum(kmax[1], n_hi)

    zero11 = jnp.zeros((1, 1), F32)
    kmax = lax.fori_loop(0, n_main, lambda c, km: fill(pl.multiple_of(c * t, t), t, km),
                         (zero11, zero11))
    kmax = fill(s_main, ATT_TAIL, kmax)
    knorm = (jnp.sqrt(kmax[0]), jnp.sqrt(kmax[1]))

    rel = (lax.broadcasted_iota(jnp.int32, (t, t), 0)
           - lax.broadcasted_iota(jnp.int32, (t, t), 1)).astype(F32)
    bias_ref[0] = jnp.zeros((t, t), F32)
    bias_ref[1] = -slope * jnp.abs(rel)

    def build_lhs(r0, n):
        rows = pos_col(r0, n)
        qn = half_rmsnorm(q_ref[pl.ds(r0, n), :], qg_ref[...]) * (1.0 / math.sqrt(HEAD_DIM))
        si_hi, si_lo = split2(slope * rows.astype(F32))
        for c in range(2):
            qc = jnp.where(lo, qn, 0.0) if c == 0 else jnp.where(lo, 0.0, qn)
            bound = jnp.sqrt(jnp.sum(qc * qc, axis=-1, keepdims=True)) * knorm[c]
            for x, sgn in enumerate((-1.0, 0.0, 1.0)):
                aug = jnp.where(lane == 0, bound,
                      jnp.where(lane == 1, sgn * si_hi,
                      jnp.where(lane == 2, sgn * si_lo,
                      jnp.where((lane == 3) | (lane == 4), -sgn,
                      jnp.where(lane == 5, 1.0, 0.0)))))
                lhs_ref[x, c * n:(c + 1) * n, 0:LANE] = qc.astype(BF16)
                lhs_ref[x, c * n:(c + 1) * n, LANE:2 * LANE] = aug.astype(BF16)

    def process(x, n, k0, nk, bias):
        s = lax.dot_general(lhs_ref[x, 0:2 * n, :], kq_ref[pl.ds(k0, nk), :], nt,
                            preferred_element_type=F32)
        if bias is not None:
            s = (s.reshape(2, n, nk) + bias[None]).reshape(2 * n, nk)
        p = jnp.exp(s).astype(BF16)
        acc_ref[0:2 * n, :] += jnp.dot(p, vq_ref[pl.ds(k0, nk), :], preferred_element_type=F32)

    def finalize(r0, n):
        a0 = acc_ref[0:n, :]
        a1 = acc_ref[n:2 * n, :]
        o = a0[:, 0:LANE] / a0[:, LANE:LANE + 1] - lam * (a1[:, 0:LANE] / a1[:, LANE:LANE + 1])
        o = _rmsnorm_rows(o, sg_ref[...]) * (1.0 - lam_init)
        o_ref[pl.ds(r0, n), :] = jnp.where(pos_col(r0, n) < seq_len, o, 0.0).astype(BF16)

    def full_chunks(qi, n):
        def body(kc, carry):
            x = jnp.where(kc < qi, 0, jnp.where(kc == qi, 1, 2))
            d = (kc == qi).astype(jnp.int32)
            process(x, n, pl.multiple_of(kc * t, t), t, bias_ref[d, 0:n, :])
            return carry
        lax.fori_loop(0, n_main, body, 0, unroll=ATT_UNROLL)

    def q_main(qi, carry):
        r0 = pl.multiple_of(qi * t, t)
        build_lhs(r0, t)
        acc_ref[...] = jnp.zeros(acc_ref.shape, F32)
        full_chunks(qi, t)
        process(2, t, s_main, ATT_TAIL, None)
        finalize(r0, t)
        return carry
    lax.fori_loop(0, n_main, q_main, 0)

    build_lhs(s_main, ATT_TAIL)
    acc_ref[...] = jnp.zeros(acc_ref.shape, F32)
    full_chunks(n_main, ATT_TAIL)
    process(1, ATT_TAIL, s_main, ATT_TAIL, bias_ref[1, 0:ATT_TAIL, 0:ATT_TAIL])
    finalize(s_main, ATT_TAIL)


def _attention_bounded(z3, scal, qg, kg, lamp, sg, seq_len):
    b, lp, _ = z3.shape
    t = _largest_tile(lp - ATT_TAIL, ATT_TILE)
    kern = functools.partial(_attn_bounded_kernel, seq_len=seq_len, t=t)
    small = lambda shape: pl.BlockSpec(shape, lambda bi, hi: (0, 0))
    col = lambda off: pl.BlockSpec((None, lp, V_DIM), lambda bi, hi: (bi, 0, off + hi))
    return pl.pallas_call(
        kern,
        grid=(b, N_HEADS),
        in_specs=[
            pl.BlockSpec(memory_space=pltpu.SMEM),
            col(0), col(N_HEADS), col(2 * N_HEADS),
            small((1, V_DIM)), small((1, V_DIM)), small((4, HEAD_DIM)), small((1, V_DIM)),
        ],
        out_specs=col(0),
        out_shape=jax.ShapeDtypeStruct((b, lp, N_HEADS * V_DIM), BF16),
        scratch_shapes=[
            pltpu.VMEM((lp, 2 * LANE), BF16), pltpu.VMEM((lp, 2 * LANE), BF16),
            pltpu.VMEM((3, 2 * t, 2 * LANE), BF16),
            pltpu.VMEM((2, t, t), F32),
            pltpu.VMEM((2 * t, 2 * LANE), F32),
        ],
        compiler_params=pltpu.CompilerParams(dimension_semantics=("parallel", "parallel"),
                                             vmem_limit_bytes=ATT_VMEM_LIMIT_BYTES),
        name="diff_attention_bounded",
    )(scal, z3, z3, z3, qg, kg, lamp, sg)


def _attention(z3, scal, qg, kg, lamp, sg, seq_len):
    span = 2.0 * HEAD_DIM / math.sqrt(HEAD_DIM) * jnp.max(jnp.abs(qg)) * jnp.max(jnp.abs(kg))
    args = (z3, scal, qg, kg, lamp, sg)
    return lax.cond(span <= SAFE_LOGIT_SPAN,
                    lambda *a: _attention_bounded(*a, seq_len),
                    lambda *a: _attention_online(*a, seq_len), *args)


def _attention_online(z3, scal, qg, kg, lamp, sg, seq_len):
    b, lp, _ = z3.shape
    tq = _largest_tile(lp, 768)
    tk = tq
    kern = functools.partial(_attn_kernel, seq_len=seq_len, tq=tq, tk=tk)
    small = lambda shape: pl.BlockSpec(shape, lambda bi, hi, qi: (0, 0))
    return pl.pallas_call(
        kern,
        grid=(b, N_HEADS, lp // tq),
        in_specs=[
            pl.BlockSpec(memory_space=pltpu.SMEM),
            pl.BlockSpec((None, tq, V_DIM), lambda bi, hi, qi: (bi, qi, hi)),
            pl.BlockSpec((None, lp, V_DIM), lambda bi, hi, qi: (bi, 0, N_HEADS + hi)),
            pl.BlockSpec((None, lp, V_DIM), lambda bi, hi, qi: (bi, 0, 2 * N_HEADS + hi)),
            small((1, V_DIM)), small((1, V_DIM)), small((4, HEAD_DIM)), small((1, V_DIM)),
        ],
        out_specs=pl.BlockSpec((None, tq, V_DIM), lambda bi, hi, qi: (bi, qi, hi)),
        out_shape=jax.ShapeDtypeStruct((b, lp, N_HEADS * V_DIM), BF16),
        scratch_shapes=[
            pltpu.VMEM((lp, V_DIM), BF16), pltpu.VMEM((lp, V_DIM), BF16),
            pltpu.VMEM((2, tq, 1), F32), pltpu.VMEM((2, tq, 1), F32),
            pltpu.VMEM((2, tq, V_DIM), F32),
        ],
        compiler_params=_cparams("parallel", "parallel", "arbitrary"),
        name="diff_attention",
    )(scal, z3, z3, z3, qg, kg, lamp, sg)


def _lru_kernel(xl_ref, gl_ref, cw_ref, cbias_ref, w_ref, ba_ref, bx_ref, lam_ref, o_ref,
                xp_ref, hf_ref, hb_ref, *, seq_len, tt):
    lp, cb = xl_ref.shape
    n = lp // tt
    pad = 8
    rows_t = lax.broadcasted_iota(jnp.int32, (tt, 1), 0)

    xp_ref[0:pad, :] = jnp.zeros((pad, cb), F32)
    xp_ref[lp + pad:lp + 2 * pad, :] = jnp.zeros((pad, cb), F32)

    def copy_in(c, carry):
        r0 = pl.multiple_of(c * tt, tt)
        x = xl_ref[pl.ds(r0, tt), :]
        xp_ref[pl.ds(r0 + pad, tt), :] = jnp.where(rows_t + r0 < seq_len, x, 0.0)
        return carry
    lax.fori_loop(0, n, copy_in, 0)

    cw = cw_ref[...]

    def gates(c, d):
        r0 = pl.multiple_of(c * tt, tt)
        win = xp_ref[pl.ds(r0, tt + 2 * pad), :]
        wn = tt + 2 * pad
        xc = (pltpu.roll(win, 2, 0)[pad:pad + tt] * cw[0:1]
              + pltpu.roll(win, 1, 0)[pad:pad + tt] * cw[1:2]
              + win[pad:pad + tt] * cw[2:3]
              + pltpu.roll(win, wn - 1, 0)[pad:pad + tt] * cw[3:4]
              + cbias_ref[...])
        pre = jnp.dot(xc.astype(BF16), w_ref[:, d * 2 * cb:(d + 1) * 2 * cb],
                      preferred_element_type=F32)
        r = jax.nn.sigmoid(pre[:, :cb] + ba_ref[d:d + 1, :])
        i = jax.nn.sigmoid(pre[:, cb:] + bx_ref[d:d + 1, :])
        nl = -lam_ref[d:d + 1, :]
        softplus = jnp.maximum(nl, 0.0) + jnp.log1p(jnp.exp(-jnp.abs(nl)))
        log_a = (-LRU_C * r) * softplus
        a = jnp.exp(log_a)
        th = jnp.tanh(log_a)
        mult = jnp.sqrt(-2.0 * th / (1.0 - th))
        u = mult * (i * xc)
        u = jnp.where(rows_t + r0 < seq_len, u, 0.0)
        return a, u

    ng = tt // SUBLANE
    sub = lax.broadcasted_iota(jnp.int32, (1, SUBLANE, 1), 1)

    def scan_chunk(a, u, carry, reverse):
        a3 = a.reshape(ng, SUBLANE, cb)
        u3 = u.reshape(ng, SUBLANE, cb)
        d = 1
        while d < SUBLANE:
            if reverse:
                shift, keep = SUBLANE - d, sub < SUBLANE - d
            else:
                shift, keep = d, sub >= d
            u3 = u3 + a3 * jnp.where(keep, pltpu.roll(u3, shift, 1), 0.0)
            a3 = a3 * jnp.where(keep, pltpu.roll(a3, shift, 1), 1.0)
            d *= 2
        hs = [None] * ng
        for g in (range(ng - 1, -1, -1) if reverse else range(ng)):
            hg = u3[g] + a3[g] * carry
            hs[g] = hg
            carry = hg[0:1, :] if reverse else hg[SUBLANE - 1:SUBLANE, :]
        return jnp.concatenate(hs, axis=0), carry

    def both(j, carry):
        cf = j
        h, state_f = scan_chunk(*gates(cf, 0), carry[0], False)
        hf_ref[pl.ds(pl.multiple_of(cf * tt, tt), tt), :] = h
        cr = n - 1 - j
        h, state_b = scan_chunk(*gates(cr, 1), carry[1], True)
        hb_ref[pl.ds(pl.multiple_of(cr * tt, tt), tt), :] = h
        return state_f, state_b
    zero_state = jnp.zeros((1, cb), F32)
    lax.fori_loop(0, n, both, (zero_state, zero_state))

    def combine(c, carry):
        r = pl.ds(pl.multiple_of(c * tt, tt), tt)
        o_ref[r, :] = ((hf_ref[r, :] + hb_ref[r, :]) * jax.nn.gelu(gl_ref[r, :])).astype(BF16)
        return carry
    lax.fori_loop(0, n, combine, 0)


def _lru(z3, cw, cbias, wcat, ba, bx, lam, seq_len):
    b, lp, _ = z3.shape
    cb = LRU_CB
    nc = D_MODEL // cb
    kern = functools.partial(_lru_kernel, seq_len=seq_len, tt=LRU_TT)
    chan = lambda rows: pl.BlockSpec((rows, cb), lambda bi, ci: (0, ci))
    return pl.pallas_call(
        kern,
        grid=(b, nc),
        in_specs=[
            pl.BlockSpec((None, lp, cb), lambda bi, ci: (bi, 0, 3 * nc + ci)),
            pl.BlockSpec((None, lp, cb), lambda bi, ci: (bi, 0, 4 * nc + ci)),
            chan(4), chan(1),
            pl.BlockSpec((None, cb, 4 * cb), lambda bi, ci: (ci, 0, 0)),
            chan(2), chan(2), chan(2),
        ],
        out_specs=pl.BlockSpec((None, lp, cb), lambda bi, ci: (bi, 0, ci)),
        out_shape=jax.ShapeDtypeStruct((b, lp, D_MODEL), BF16),
        scratch_shapes=[pltpu.VMEM((lp + 16, cb), F32), pltpu.VMEM((lp, cb), F32),
                        pltpu.VMEM((lp, cb), F32)],
        compiler_params=_cparams("parallel", "parallel"),
        name="rglru",
    )(z3, z3, cw, cbias, wcat, ba, bx, lam)


def _merge_kernel(att_ref, yl_ref, ga_ref, gr_ref, h_ref, wa_ref, wl_ref, wo_ref,
                  bga_ref, bgr_ref, o_ref):
    ya = jnp.dot(att_ref[...], wa_ref[...], preferred_element_type=F32)
    yr = jnp.dot(yl_ref[...], wl_ref[...], preferred_element_type=F32)
    merged = (jax.nn.sigmoid(ga_ref[...] + bga_ref[...]) * ya
              + jax.nn.sigmoid(gr_ref[...] + bgr_ref[...]) * yr)
    o_ref[...] = h_ref[...] + jnp.dot(merged.astype(BF16), wo_ref[...],
                                      preferred_element_type=F32)


def _merge(att2, yl2, z2, h2, wa, wl, wo, bga, bgr):
    m = h2.shape[0]
    tm = _largest_tile(m, 256)
    row = lambda col: pl.BlockSpec((tm, D_MODEL), lambda i: (i, col))
    full = lambda shape: pl.BlockSpec(shape, lambda i: (0, 0))
    return pl.pallas_call(
        _merge_kernel,
        grid=(m // tm,),
        in_specs=[row(0), row(0), row(5), row(6), row(0),
                  full((D_MODEL, D_MODEL)), full((D_MODEL, D_MODEL)), full((D_MODEL, D_MODEL)),
                  full((1, D_MODEL)), full((1, D_MODEL))],
        out_specs=row(0),
        out_shape=jax.ShapeDtypeStruct((m, D_MODEL), F32),
        compiler_params=_cparams("parallel"),
        name="merge_out_proj",
    )(att2, yl2, z2, z2, h2, wa, wl, wo, bga, bgr)


def _ffn_kernel(*refs, chunks_per_expert):
    if chunks_per_expert:
        h_ref, g_ref, wr_ref, wg_ref, wu_ref, wd_ref, o_ref, hn_ref, acc_ref, gate_ref = refs
    else:
        h_ref, g_ref, wg_ref, wu_ref, wd_ref, o_ref, hn_ref, acc_ref = refs
    j = pl.program_id(1)

    @pl.when(j == 0)
    def _():
        h = h_ref[...]
        hn = _rmsnorm_rows(h, g_ref[...])
        hn_ref[...] = hn.astype(BF16)
        acc_ref[...] = h
        if chunks_per_expert:
            logits = jnp.dot(hn, wr_ref[...], preferred_element_type=F32,
                             precision=lax.Precision.HIGHEST)
            lane = lax.broadcasted_iota(jnp.int32, logits.shape, 1).astype(F32)
            lg = jnp.where(lane < N_EXPERTS, logits, NEG_BIG)
            m1 = jnp.max(lg, axis=-1, keepdims=True)
            i1 = jnp.min(jnp.where(lg == m1, lane, float(LANE)), axis=-1, keepdims=True)
            lg2 = jnp.where(lane == i1, NEG_BIG, lg)
            m2 = jnp.max(lg2, axis=-1, keepdims=True)
            i2 = jnp.min(jnp.where(lg2 == m2, lane, float(LANE)), axis=-1, keepdims=True)
            e2 = jnp.exp(m2 - m1)
            den = 1.0 + e2
            gate_ref[...] = jnp.where(lane == i1, 1.0 / den,
                                      jnp.where(lane == i2, e2 / den, 0.0))

    hn = hn_ref[...]
    gg = jnp.dot(hn, wg_ref[...], preferred_element_type=F32)
    uu = jnp.dot(hn, wu_ref[...], preferred_element_type=F32)
    act = (jax.nn.silu(gg) * uu).astype(BF16)
    y = jnp.dot(act, wd_ref[...], preferred_element_type=F32)
    if chunks_per_expert:
        e = (j // chunks_per_expert).astype(F32)
        gates = gate_ref[...]
        lane = lax.broadcasted_iota(jnp.int32, gates.shape, 1).astype(F32)
        y = y * jnp.sum(jnp.where(lane == e, gates, 0.0), axis=-1, keepdims=True)
    acc_ref[...] += y

    @pl.when(j == pl.num_programs(1) - 1)
    def _():
        o_ref[...] = acc_ref[...]


def _ffn(h2, g, wg, wu, wd, router=None, d_ff_expert=None):
    m = h2.shape[0]
    f = wg.shape[1]
    tm = _largest_tile(m, 512)
    tf = 512
    in_specs = [pl.BlockSpec((tm, D_MODEL), lambda i, j: (i, 0)),
                pl.BlockSpec((1, D_MODEL), lambda i, j: (0, 0))]
    args = [h2, g]
    scratch = [pltpu.VMEM((tm, D_MODEL), BF16), pltpu.VMEM((tm, D_MODEL), F32)]
    cpe = 0
    if router is not None:
        assert d_ff_expert % tf == 0
        cpe = d_ff_expert // tf
        in_specs.append(pl.BlockSpec((D_MODEL, LANE), lambda i, j: (0, 0)))
        args.append(router)
        scratch.append(pltpu.VMEM((tm, LANE), F32))
    in_specs += [pl.BlockSpec((D_MODEL, tf), lambda i, j: (0, j)),
                 pl.BlockSpec((D_MODEL, tf), lambda i, j: (0, j)),
                 pl.BlockSpec((tf, D_MODEL), lambda i, j: (j, 0))]
    args += [wg, wu, wd]
    return pl.pallas_call(
        functools.partial(_ffn_kernel, chunks_per_expert=cpe),
        grid=(m // tm, f // tf),
        in_specs=in_specs,
        out_specs=pl.BlockSpec((tm, D_MODEL), lambda i, j: (i, 0)),
        out_shape=jax.ShapeDtypeStruct((m, D_MODEL), F32),
        scratch_shapes=scratch,
        compiler_params=_cparams("parallel", "arbitrary"),
        name="moe_ffn" if cpe else "dense_ffn",
    )(*args)


def _blockdiag_groups(w, per):
    g = LRU_BLOCKS // per
    w = w.reshape(g, per, LRU_BLOCK_W, LRU_BLOCK_W)
    eye = jnp.eye(per, dtype=w.dtype)
    out = jnp.einsum('gpio,pq->gpiqo', w, eye)
    return out.reshape(g, per * LRU_BLOCK_W, per * LRU_BLOCK_W)


def _prepare(p):
    depth = p['w_in'].shape[0]
    per = LRU_CB // LRU_BLOCK_W
    layers = []
    slopes = jnp.exp2(-8.0 * jnp.arange(1, N_HEADS + 1, dtype=F32) / N_HEADS)
    for l in range(depth):
        lam_init = 0.8 - 0.6 * math.exp(-0.3 * l)
        wcat = jnp.concatenate(
            [_blockdiag_groups(p['lru_wa'][l, 0], per), _blockdiag_groups(p['lru_wx'][l, 0], per),
             _blockdiag_groups(p['lru_wa'][l, 1], per), _blockdiag_groups(p['lru_wx'][l, 1], per)],
            axis=-1).astype(BF16)
        lay = dict(
            norm_mix=p['norm_mix'][l][None], norm_ffn=p['norm_ffn'][l][None],
            w_in=p['w_in'][l].astype(BF16),
            scal=jnp.concatenate([slopes, jnp.full((N_HEADS,), lam_init, F32)]),
            qg=jnp.tile(p['q_norm'][l], 2)[None], kg=jnp.tile(p['k_norm'][l], 2)[None],
            lamp=jnp.stack([p['lambda_q1'][l], p['lambda_k1'][l],
                            p['lambda_q2'][l], p['lambda_k2'][l]]),
            sg=p['attn_subln'][l][None],
            conv_w=p['conv_w'][l], conv_b=p['conv_b'][l][None], wcat=wcat,
            ba=p['lru_ba'][l], bx=p['lru_bx'][l], lam=p['lru_lambda'][l],
            wa=p['w_attn_branch'][l].astype(BF16), wl=p['w_lru_branch'][l].astype(BF16),
            wo=p['w_out'][l].astype(BF16),
            bga=p['b_gate'][l, :D_MODEL][None], bgr=p['b_gate'][l, D_MODEL:][None],
        )
        j = l // 2
        if l % 2 == 0:
            lay.update(wg=p['ffn_w_gate'][j].astype(BF16), wu=p['ffn_w_up'][j].astype(BF16),
                       wd=p['ffn_w_down'][j].astype(BF16))
        else:
            ne, _, fe = p['moe_w_gate'][j].shape
            lay.update(
                wg=jnp.moveaxis(p['moe_w_gate'][j], 0, 1).reshape(D_MODEL, ne * fe).astype(BF16),
                wu=jnp.moveaxis(p['moe_w_up'][j], 0, 1).reshape(D_MODEL, ne * fe).astype(BF16),
                wd=p['moe_w_down'][j].reshape(ne * fe, D_MODEL).astype(BF16),
                router=jnp.pad(p['moe_router'][j], ((0, 0), (0, LANE - ne))),
                d_ff_expert=fe)
        layers.append(lay)
    return layers


def _trunk(x, meta, layers):
    b, s, _ = x.shape
    seq_len = s + N_META
    lp = -(-seq_len // LANE) * LANE
    h = jnp.concatenate(
        [jnp.broadcast_to(meta[None].astype(x.dtype), (b, N_META, D_MODEL)), x,
         jnp.zeros((b, lp - seq_len, D_MODEL), x.dtype)], axis=1)
    h2 = h.reshape(b * lp, D_MODEL)
    for lay in layers:
        z2 = _in_proj(h2, lay['norm_mix'], lay['w_in'])
        z3 = z2.reshape(b, lp, IN_WIDTH)
        att = _attention(z3, lay['scal'], lay['qg'], lay['kg'], lay['lamp'], lay['sg'], seq_len)
        yl = _lru(z3, lay['conv_w'], lay['conv_b'], lay['wcat'], lay['ba'], lay['bx'],
                  lay['lam'], seq_len)
        h2 = _merge(att.reshape(b * lp, D_MODEL), yl.reshape(b * lp, D_MODEL), z2, h2,
                    lay['wa'], lay['wl'], lay['wo'], lay['bga'], lay['bgr'])
        h2 = _ffn(h2, lay['norm_ffn'], lay['wg'], lay['wu'], lay['wd'],
                  lay.get('router'), lay.get('d_ff_expert'))
    return h2.reshape(b, lp, D_MODEL)[:, N_META:seq_len]


def kernel(x_prompt, x_sample, meta_tokens, norm_mix, norm_ffn, w_in, b_gate, q_norm, k_norm, lambda_q1, lambda_k1, lambda_q2, lambda_k2, attn_subln, w_attn_branch, conv_w, conv_b, lru_wa, lru_ba, lru_wx, lru_bx, lru_lambda, w_lru_branch, w_out, ffn_w_gate, ffn_w_up, ffn_w_down, moe_router, moe_w_gate, moe_w_up, moe_w_down):
    p = dict(norm_mix=norm_mix, norm_ffn=norm_ffn, w_in=w_in, b_gate=b_gate, q_norm=q_norm,
             k_norm=k_norm, lambda_q1=lambda_q1, lambda_k1=lambda_k1, lambda_q2=lambda_q2,
             lambda_k2=lambda_k2, attn_subln=attn_subln, w_attn_branch=w_attn_branch,
             conv_w=conv_w, conv_b=conv_b, lru_wa=lru_wa, lru_ba=lru_ba, lru_wx=lru_wx,
             lru_bx=lru_bx, lru_lambda=lru_lambda, w_lru_branch=w_lru_branch, w_out=w_out,
             ffn_w_gate=ffn_w_gate, ffn_w_up=ffn_w_up, ffn_w_down=ffn_w_down,
             moe_router=moe_router, moe_w_gate=moe_w_gate, moe_w_up=moe_w_up,
             moe_w_down=moe_w_down)
    layers = _prepare(p)
    return (_trunk(x_prompt, meta_tokens, layers), _trunk(x_sample, meta_tokens, layers))
```

```python
import functools
import math

import jax
import jax.numpy as jnp
from jax import lax
from jax.experimental import pallas as pl
from jax.experimental.pallas import tpu as pltpu

F32 = jnp.float32
BF16 = jnp.bfloat16

D_MODEL = 1024
N_META = 16
N_HEADS = 8
HEAD_DIM = 64
V_DIM = 2 * HEAD_DIM
LRU_BLOCKS = 16
LRU_BLOCK_W = D_MODEL // LRU_BLOCKS
LRU_C = 8.0
N_EXPERTS = 8
EPS = 1e-6
IN_WIDTH = 7 * D_MODEL

LANE = 128
SUBLANE = 8
VMEM_LIMIT_BYTES = 48 * 1024 * 1024
NEG_BIG = -1e30

ATT_TILE = 512
ATT_MAX_UNROLL = 8
EXP_ZERO_LOGIT = 105.0
ATT_TAIL = LANE
ATT_VMEM_LIMIT_BYTES = 56 * 1024 * 1024
MASK_LOGIT = -30000.0
SAFE_LOGIT_SPAN = 80.0

LRU_CB = 128
LRU_TT = 128


def _largest_tile(n, target, mult=LANE):
    assert n % mult == 0, (n, mult)
    best = mult
    t = mult
    while t <= min(n, target):
        if n % t == 0:
            best = t
        t += mult
    return best


def _cparams(*sem):
    return pltpu.CompilerParams(dimension_semantics=sem, vmem_limit_bytes=VMEM_LIMIT_BYTES)


def _rmsnorm_rows(x, g):
    ms = jnp.mean(x * x, axis=-1, keepdims=True)
    return x * lax.rsqrt(ms + EPS) * g


def _in_proj_kernel(h_ref, g_ref, w_ref, z_ref, xn_ref):
    @pl.when(pl.program_id(1) == 0)
    def _():
        xn_ref[...] = _rmsnorm_rows(h_ref[...], g_ref[...]).astype(BF16)

    z_ref[...] = jnp.dot(xn_ref[...], w_ref[...], preferred_element_type=F32)


def _in_proj(h2, g, w):
    m = h2.shape[0]
    tm = _largest_tile(m, 1024)
    tn = _largest_tile(IN_WIDTH, 1792, 2 * LANE)
    return pl.pallas_call(
        _in_proj_kernel,
        grid=(m // tm, IN_WIDTH // tn),
        in_specs=[
            pl.BlockSpec((tm, D_MODEL), lambda i, j: (i, 0)),
            pl.BlockSpec((1, D_MODEL), lambda i, j: (0, 0)),
            pl.BlockSpec((D_MODEL, tn), lambda i, j: (0, j)),
        ],
        out_specs=pl.BlockSpec((tm, tn), lambda i, j: (i, j)),
        out_shape=jax.ShapeDtypeStruct((m, IN_WIDTH), F32),
        scratch_shapes=[pltpu.VMEM((tm, D_MODEL), BF16)],
        compiler_params=_cparams("parallel", "arbitrary"),
        name="in_proj",
    )(h2, g, w)


def _attn_kernel(sc_ref, q_ref, k_ref, v_ref, qg_ref, kg_ref, lam_ref, sg_ref, o_ref,
                 kb_ref, vb_ref, m_ref, l_ref, acc_ref, *, seq_len, tq, tk):
    lp = k_ref.shape[0]
    nk = lp // tk
    head = pl.program_id(1)
    qi = pl.program_id(2)
    lane = lax.broadcasted_iota(jnp.int32, (1, LANE), 1)
    lo = lane < HEAD_DIM

    def half_rmsnorm(x, g):
        sq = x * x
        s_lo = jnp.sum(jnp.where(lo, sq, 0.0), axis=-1, keepdims=True)
        s_hi = jnp.sum(jnp.where(lo, 0.0, sq), axis=-1, keepdims=True)
        ms = jnp.where(lo, s_lo, s_hi) * (1.0 / HEAD_DIM)
        return x * lax.rsqrt(ms + EPS) * g

    @pl.when(qi == 0)
    def _():
        def fill(c, carry):
            r = pl.ds(pl.multiple_of(c * tk, tk), tk)
            kb_ref[r, :] = half_rmsnorm(k_ref[r, :], kg_ref[...]).astype(BF16)
            vb_ref[r, :] = v_ref[r, :].astype(BF16)
            return carry
        lax.fori_loop(0, nk, fill, 0)

    qn = half_rmsnorm(q_ref[...], qg_ref[...]) * (1.0 / math.sqrt(HEAD_DIM))
    q_comp = (jnp.where(lo, qn, 0.0).astype(BF16), jnp.where(lo, 0.0, qn).astype(BF16))

    m_ref[...] = jnp.full(m_ref.shape, NEG_BIG, F32)
    l_ref[...] = jnp.zeros(l_ref.shape, F32)
    acc_ref[...] = jnp.zeros(acc_ref.shape, F32)

    slope = sc_ref[head]
    rel = (lax.broadcasted_iota(jnp.int32, (tq, tk), 0)
           - lax.broadcasted_iota(jnp.int32, (tq, tk), 1)).astype(F32)
    col = lax.broadcasted_iota(jnp.int32, (1, tk), 1)

    def chunk(kc, carry):
        k0 = pl.multiple_of(kc * tk, tk)
        kblk = kb_ref[pl.ds(k0, tk), :]
        vblk = vb_ref[pl.ds(k0, tk), :]
        off = (qi * tq - k0).astype(F32)
        bias = -slope * jnp.abs(rel + off) + jnp.where(col + k0 < seq_len, 0.0, NEG_BIG)
        for c in range(2):
            s = lax.dot_general(q_comp[c], kblk, (((1,), (1,)), ((), ())),
                                preferred_element_type=F32) + bias
            m_prev = m_ref[c]
            m_new = jnp.maximum(m_prev, jnp.max(s, axis=-1, keepdims=True))
            alpha = jnp.exp(m_prev - m_new)
            p = jnp.exp(s - m_new)
            l_ref[c] = alpha * l_ref[c] + jnp.sum(p, axis=-1, keepdims=True)
            acc_ref[c] = alpha * acc_ref[c] + jnp.dot(p.astype(BF16), vblk,
                                                      preferred_element_type=F32)
            m_ref[c] = m_new
        return carry

    lax.fori_loop(0, nk, chunk, 0)

    lam_init = sc_ref[N_HEADS]
    lp_ = lam_ref[...]
    lam = (jnp.exp(jnp.sum(lp_[0:1] * lp_[1:2], axis=-1, keepdims=True))
           - jnp.exp(jnp.sum(lp_[2:3] * lp_[3:4], axis=-1, keepdims=True)) + lam_init)
    o = acc_ref[0] / l_ref[0] - lam * (acc_ref[1] / l_ref[1])
    o = _rmsnorm_rows(o, sg_ref[...]) * (1.0 - lam_init)
    o_ref[...] = o.astype(BF16)


def _attn_bounded_kernel(sc_ref, q_ref, k_ref, v_ref, qg_ref, kg_ref, lam_ref, sg_ref, prev_ref,
                         o_ref, kq_ref, vq_ref, lhs_ref, bias_ref, acc_ref,
                         *, seq_len, t, head0, reach):
    lp = k_ref.shape[0]
    s_main = lp - ATT_TAIL
    n_main = s_main // t
    del prev_ref
    slope = sc_ref[head0 + pl.program_id(1)]
    lam_init = sc_ref[N_HEADS]
    lane = lax.broadcasted_iota(jnp.int32, (1, LANE), 1)
    lo = lane < HEAD_DIM
    nt = (((1,), (1,)), ((), ()))

    def half_rmsnorm(x, g):
        sq = x * x
        s_lo = jnp.sum(jnp.where(lo, sq, 0.0), axis=-1, keepdims=True)
        s_hi = jnp.sum(jnp.where(lo, 0.0, sq), axis=-1, keepdims=True)
        ms = jnp.where(lo, s_lo, s_hi) * (1.0 / HEAD_DIM)
        return x * lax.rsqrt(ms + EPS) * g

    def split2(x):
        hi = x.astype(BF16).astype(F32)
        return hi, x - hi

    def pos_col(r0, n):
        return lax.broadcasted_iota(jnp.int32, (n, 1), 0) + r0

    lp_ = lam_ref[...]
    lam = (jnp.exp(jnp.sum(lp_[0:1] * lp_[1:2], axis=-1, keepdims=True))
           - jnp.exp(jnp.sum(lp_[2:3] * lp_[3:4], axis=-1, keepdims=True)) + lam_init)

    def fill(r0, n, kmax):
        rows = pos_col(r0, n)
        valid = rows < seq_len
        kn = jnp.where(valid, half_rmsnorm(k_ref[pl.ds(r0, n), :], kg_ref[...]), 0.0)
        sq = kn * kn
        n_lo = jnp.max(jnp.sum(jnp.where(lo, sq, 0.0), axis=-1, keepdims=True), axis=0, keepdims=True)
        n_hi = jnp.max(jnp.sum(jnp.where(lo, 0.0, sq), axis=-1, keepdims=True), axis=0, keepdims=True)
        sj_hi, sj_lo = split2(slope * rows.astype(F32))
        aug = jnp.where(lane == 0, -1.0,
              jnp.where((lane == 1) | (lane == 2), 1.0,
              jnp.where(lane == 3, sj_hi,
              jnp.where(lane == 4, sj_lo,
              jnp.where(lane == 5, jnp.where(valid, 0.0, MASK_LOGIT), 0.0)))))
        kq_ref[pl.ds(r0, n), 0:LANE] = kn.astype(BF16)
        kq_ref[pl.ds(r0, n), LANE:2 * LANE] = aug.astype(BF16)
        vq_ref[pl.ds(r0, n), 0:LANE] = jnp.where(valid, v_ref[pl.ds(r0, n), :], 0.0).astype(BF16)
        vq_ref[pl.ds(r0, n), LANE:2 * LANE] = jnp.broadcast_to(
            jnp.where(lane == 0, 1.0, 0.0), (n, LANE)).astype(BF16)
        return jnp.maximum(kmax[0], n_lo), jnp.maximum(kmax[1], n_hi)

    zero11 = jnp.zeros((1, 1), F32)
    kmax = lax.fori_loop(0, n_main, lambda c, km: fill(pl.multiple_of(c * t, t), t, km),
                         (zero11, zero11))
    kmax = fill(s_main, ATT_TAIL, kmax)
    knorm = (jnp.sqrt(kmax[0]), jnp.sqrt(kmax[1]))

    rel = (lax.broadcasted_iota(jnp.int32, (t, t), 0)
           - lax.broadcasted_iota(jnp.int32, (t, t), 1)).astype(F32)
    bias_ref[0] = jnp.zeros((t, t), F32)
    bias_ref[1] = -slope * jnp.abs(rel)

    def build_lhs(r0, n):
        rows = pos_col(r0, n)
        qn = half_rmsnorm(q_ref[pl.ds(r0, n), :], qg_ref[...]) * (1.0 / math.sqrt(HEAD_DIM))
        si_hi, si_lo = split2(slope * rows.astype(F32))
        for c in range(2):
            qc = jnp.where(lo, qn, 0.0) if c == 0 else jnp.where(lo, 0.0, qn)
            bound = jnp.sqrt(jnp.sum(qc * qc, axis=-1, keepdims=True)) * knorm[c]
            for x, sgn in enumerate((-1.0, 0.0, 1.0)):
                aug = jnp.where(lane == 0, bound,
                      jnp.where(lane == 1, sgn * si_hi,
                      jnp.where(lane == 2, sgn * si_lo,
                      jnp.where((lane == 3) | (lane == 4), -sgn,
                      jnp.where(lane == 5, 1.0, 0.0)))))
                lhs_ref[x, c * n:(c + 1) * n, 0:LANE] = qc.astype(BF16)
                lhs_ref[x, c * n:(c + 1) * n, LANE:2 * LANE] = aug.astype(BF16)

    def process(x, n, k0, nk, bias):
        s = lax.dot_general(lhs_ref[x, 0:2 * n, :], kq_ref[pl.ds(k0, nk), :], nt,
                            preferred_element_type=F32)
        if bias is not None:
            s = (s.reshape(2, n, nk) + bias[None]).reshape(2 * n, nk)
        p = jnp.exp(s).astype(BF16)
        acc_ref[0:2 * n, :] += jnp.dot(p, vq_ref[pl.ds(k0, nk), :], preferred_element_type=F32)

    def finalize(r0, n):
        a0 = acc_ref[0:n, :]
        a1 = acc_ref[n:2 * n, :]
        o = a0[:, 0:LANE] / a0[:, LANE:LANE + 1] - lam * (a1[:, 0:LANE] / a1[:, LANE:LANE + 1])
        o = _rmsnorm_rows(o, sg_ref[...]) * (1.0 - lam_init)
        o_ref[pl.ds(r0, n), :] = jnp.where(pos_col(r0, n) < seq_len, o, 0.0).astype(BF16)

    window = min(2 * reach + 1, n_main)

    def full_chunks(qi, n, kc0, count):
        def body(w, carry):
            kc = kc0 + w
            x = jnp.where(kc < qi, 0, jnp.where(kc == qi, 1, 2))
            d = jnp.where(kc == qi, 1, 0)
            process(x, n, pl.multiple_of(kc * t, t), t, bias_ref[d, 0:n, :])
            return carry
        lax.fori_loop(0, count, body, 0, unroll=min(count, ATT_MAX_UNROLL))

    def q_main(qi, carry):
        r0 = pl.multiple_of(qi * t, t)
        build_lhs(r0, t)
        acc_ref[...] = jnp.zeros(acc_ref.shape, F32)
        full_chunks(qi, t, jnp.clip(qi - reach, 0, n_main - window), window)

        @pl.when(n_main - qi <= reach)
        def _():
            process(2, t, s_main, ATT_TAIL, None)
        finalize(r0, t)
        return carry
    lax.fori_loop(0, n_main, q_main, 0)

    build_lhs(s_main, ATT_TAIL)
    acc_ref[...] = jnp.zeros(acc_ref.shape, F32)
    tail_count = min(reach, n_main)
    full_chunks(n_main, ATT_TAIL, n_main - tail_count, tail_count)
    process(1, ATT_TAIL, s_main, ATT_TAIL, bias_ref[1, 0:ATT_TAIL, 0:ATT_TAIL])
    finalize(s_main, ATT_TAIL)


def _chunk_reach(head, t):
    slope = 2.0 ** (-8.0 * (head + 1) / N_HEADS)
    return max(1, math.ceil((EXP_ZERO_LOGIT / slope - 1.0) / t))


def _attention_bounded(z3, scal, qg, kg, lamp, sg, seq_len):
    b, lp, _ = z3.shape
    t = _largest_tile(lp - ATT_TAIL, ATT_TILE)
    n_main = (lp - ATT_TAIL) // t
    groups = []
    for head in range(N_HEADS):
        reach = min(_chunk_reach(head, t), n_main)
        if groups and min(2 * groups[-1][2] + 1, n_main) == min(2 * reach + 1, n_main):
            groups[-1][1] += 1
            groups[-1][2] = max(groups[-1][2], reach)
        else:
            groups.append([head, 1, reach])
    small = lambda shape: pl.BlockSpec(shape, lambda bi, hi: (0, 0))
    att = jnp.zeros((b, lp, N_HEADS * V_DIM), BF16)
    for head0, count, reach in groups:
        kern = functools.partial(_attn_bounded_kernel, seq_len=seq_len, t=t, head0=head0,
                                 reach=reach)
        col = lambda off, head0=head0: pl.BlockSpec(
            (None, lp, V_DIM), lambda bi, hi: (bi, 0, off + head0 + hi))
        att = pl.pallas_call(
            kern,
            grid=(b, count),
            in_specs=[
                pl.BlockSpec(memory_space=pltpu.SMEM),
                col(0), col(N_HEADS), col(2 * N_HEADS),
                small((1, V_DIM)), small((1, V_DIM)), small((4, HEAD_DIM)), small((1, V_DIM)),
                pl.BlockSpec(memory_space=pl.ANY),
            ],
            out_specs=col(0),
            out_shape=jax.ShapeDtypeStruct(att.shape, att.dtype),
            input_output_aliases={8: 0},
            scratch_shapes=[
                pltpu.VMEM((lp, 2 * LANE), BF16), pltpu.VMEM((lp, 2 * LANE), BF16),
                pltpu.VMEM((3, 2 * t, 2 * LANE), BF16),
                pltpu.VMEM((2, t, t), F32),
                pltpu.VMEM((2 * t, 2 * LANE), F32),
            ],
            compiler_params=pltpu.CompilerParams(dimension_semantics=("parallel", "parallel"),
                                                 vmem_limit_bytes=ATT_VMEM_LIMIT_BYTES),
            name="diff_attention_bounded",
        )(scal, z3, z3, z3, qg, kg, lamp, sg, att)
    return att


def _attention(z3, scal, qg, kg, lamp, sg, seq_len):
    span = 2.0 * HEAD_DIM / math.sqrt(HEAD_DIM) * jnp.max(jnp.abs(qg)) * jnp.max(jnp.abs(kg))
    args = (z3, scal, qg, kg, lamp, sg)
    return lax.cond(span <= SAFE_LOGIT_SPAN,
                    lambda *a: _attention_bounded(*a, seq_len),
                    lambda *a: _attention_online(*a, seq_len), *args)


def _attention_online(z3, scal, qg, kg, lamp, sg, seq_len):
    b, lp, _ = z3.shape
    tq = _largest_tile(lp, 768)
    tk = tq
    kern = functools.partial(_attn_kernel, seq_len=seq_len, tq=tq, tk=tk)
    small = lambda shape: pl.BlockSpec(shape, lambda bi, hi, qi: (0, 0))
    return pl.pallas_call(
        kern,
        grid=(b, N_HEADS, lp // tq),
        in_specs=[
            pl.BlockSpec(memory_space=pltpu.SMEM),
            pl.BlockSpec((None, tq, V_DIM), lambda bi, hi, qi: (bi, qi, hi)),
            pl.BlockSpec((None, lp, V_DIM), lambda bi, hi, qi: (bi, 0, N_HEADS + hi)),
            pl.BlockSpec((None, lp, V_DIM), lambda bi, hi, qi: (bi, 0, 2 * N_HEADS + hi)),
            small((1, V_DIM)), small((1, V_DIM)), small((4, HEAD_DIM)), small((1, V_DIM)),
        ],
        out_specs=pl.BlockSpec((None, tq, V_DIM), lambda bi, hi, qi: (bi, qi, hi)),
        out_shape=jax.ShapeDtypeStruct((b, lp, N_HEADS * V_DIM), BF16),
        scratch_shapes=[
            pltpu.VMEM((lp, V_DIM), BF16), pltpu.VMEM((lp, V_DIM), BF16),
            pltpu.VMEM((2, tq, 1), F32), pltpu.VMEM((2, tq, 1), F32),
            pltpu.VMEM((2, tq, V_DIM), F32),
        ],
        compiler_params=_cparams("parallel", "parallel", "arbitrary"),
        name="diff_attention",
    )(scal, z3, z3, z3, qg, kg, lamp, sg)


def _lru_kernel(xl_ref, gl_ref, cw_ref, cbias_ref, w_ref, ba_ref, bx_ref, lam_ref, o_ref,
                xp_ref, hf_ref, hb_ref, *, seq_len, tt):
    lp, cb = xl_ref.shape
    n = lp // tt
    pad = 8
    rows_t = lax.broadcasted_iota(jnp.int32, (tt, 1), 0)

    xp_ref[0:pad, :] = jnp.zeros((pad, cb), F32)
    xp_ref[lp + pad:lp + 2 * pad, :] = jnp.zeros((pad, cb), F32)

    def copy_in(c, carry):
        r0 = pl.multiple_of(c * tt, tt)
        x = xl_ref[pl.ds(r0, tt), :]
        xp_ref[pl.ds(r0 + pad, tt), :] = jnp.where(rows_t + r0 < seq_len, x, 0.0)
        return carry
    lax.fori_loop(0, n, copy_in, 0)

    cw = cw_ref[...]

    def gates(c, d):
        r0 = pl.multiple_of(c * tt, tt)
        win = xp_ref[pl.ds(r0, tt + 2 * pad), :]
        wn = tt + 2 * pad
        xc = (pltpu.roll(win, 2, 0)[pad:pad + tt] * cw[0:1]
              + pltpu.roll(win, 1, 0)[pad:pad + tt] * cw[1:2]
              + win[pad:pad + tt] * cw[2:3]
              + pltpu.roll(win, wn - 1, 0)[pad:pad + tt] * cw[3:4]
              + cbias_ref[...])
        pre = jnp.dot(xc.astype(BF16), w_ref[:, d * 2 * cb:(d + 1) * 2 * cb],
                      preferred_element_type=F32)
        r = jax.nn.sigmoid(pre[:, :cb] + ba_ref[d:d + 1, :])
        i = jax.nn.sigmoid(pre[:, cb:] + bx_ref[d:d + 1, :])
        nl = -lam_ref[d:d + 1, :]
        softplus = jnp.maximum(nl, 0.0) + jnp.log1p(jnp.exp(-jnp.abs(nl)))
        log_a = (-LRU_C * r) * softplus
        a = jnp.exp(log_a)
        th = jnp.tanh(log_a)
        mult = jnp.sqrt(-2.0 * th / (1.0 - th))
        u = mult * (i * xc)
        u = jnp.where(rows_t + r0 < seq_len, u, 0.0)
        return a, u

    ng = tt // SUBLANE
    sub = lax.broadcasted_iota(jnp.int32, (1, SUBLANE, 1), 1)

    def scan_chunk(a, u, carry, reverse):
        a3 = a.reshape(ng, SUBLANE, cb)
        u3 = u.reshape(ng, SUBLANE, cb)
        d = 1
        while d < SUBLANE:
            if reverse:
                shift, keep = SUBLANE - d, sub < SUBLANE - d
            else:
                shift, keep = d, sub >= d
            u3 = u3 + a3 * jnp.where(keep, pltpu.roll(u3, shift, 1), 0.0)
            a3 = a3 * jnp.where(keep, pltpu.roll(a3, shift, 1), 1.0)
            d *= 2
        hs = [None] * ng
        for g in (range(ng - 1, -1, -1) if reverse else range(ng)):
            hg = u3[g] + a3[g] * carry
            hs[g] = hg
            carry = hg[0:1, :] if reverse else hg[SUBLANE - 1:SUBLANE, :]
        return jnp.concatenate(hs, axis=0), carry

    def both(j, carry):
        cf = j
        h, state_f = scan_chunk(*gates(cf, 0), carry[0], False)
        hf_ref[pl.ds(pl.multiple_of(cf * tt, tt), tt), :] = h
        cr = n - 1 - j
        h, state_b = scan_chunk(*gates(cr, 1), carry[1], True)
        hb_ref[pl.ds(pl.multiple_of(cr * tt, tt), tt), :] = h
        return state_f, state_b
    zero_state = jnp.zeros((1, cb), F32)
    lax.fori_loop(0, n, both, (zero_state, zero_state))

    def combine(c, carry):
        r = pl.ds(pl.multiple_of(c * tt, tt), tt)
        o_ref[r, :] = ((hf_ref[r, :] + hb_ref[r, :]) * jax.nn.gelu(gl_ref[r, :])).astype(BF16)
        return carry
    lax.fori_loop(0, n, combine, 0)


def _lru(z3, cw, cbias, wcat, ba, bx, lam, seq_len):
    b, lp, _ = z3.shape
    cb = LRU_CB
    nc = D_MODEL // cb
    kern = functools.partial(_lru_kernel, seq_len=seq_len, tt=LRU_TT)
    chan = lambda rows: pl.BlockSpec((rows, cb), lambda bi, ci: (0, ci))
    return pl.pallas_call(
        kern,
        grid=(b, nc),
        in_specs=[
            pl.BlockSpec((None, lp, cb), lambda bi, ci: (bi, 0, 3 * nc + ci)),
            pl.BlockSpec((None, lp, cb), lambda bi, ci: (bi, 0, 4 * nc + ci)),
            chan(4), chan(1),
            pl.BlockSpec((None, cb, 4 * cb), lambda bi, ci: (ci, 0, 0)),
            chan(2), chan(2), chan(2),
        ],
        out_specs=pl.BlockSpec((None, lp, cb), lambda bi, ci: (bi, 0, ci)),
        out_shape=jax.ShapeDtypeStruct((b, lp, D_MODEL), BF16),
        scratch_shapes=[pltpu.VMEM((lp + 16, cb), F32), pltpu.VMEM((lp, cb), F32),
                        pltpu.VMEM((lp, cb), F32)],
        compiler_params=_cparams("parallel", "parallel"),
        name="rglru",
    )(z3, z3, cw, cbias, wcat, ba, bx, lam)


def _merge_kernel(att_ref, yl_ref, ga_ref, gr_ref, h_ref, wa_ref, wl_ref, wo_ref,
                  bga_ref, bgr_ref, o_ref):
    ya = jnp.dot(att_ref[...], wa_ref[...], preferred_element_type=F32)
    yr = jnp.dot(yl_ref[...], wl_ref[...], preferred_element_type=F32)
    merged = (jax.nn.sigmoid(ga_ref[...] + bga_ref[...]) * ya
              + jax.nn.sigmoid(gr_ref[...] + bgr_ref[...]) * yr)
    o_ref[...] = h_ref[...] + jnp.dot(merged.astype(BF16), wo_ref[...],
                                      preferred_element_type=F32)


def _merge(att2, yl2, z2, h2, wa, wl, wo, bga, bgr):
    m = h2.shape[0]
    tm = _largest_tile(m, 256)
    row = lambda col: pl.BlockSpec((tm, D_MODEL), lambda i: (i, col))
    full = lambda shape: pl.BlockSpec(shape, lambda i: (0, 0))
    return pl.pallas_call(
        _merge_kernel,
        grid=(m // tm,),
        in_specs=[row(0), row(0), row(5), row(6), row(0),
                  full((D_MODEL, D_MODEL)), full((D_MODEL, D_MODEL)), full((D_MODEL, D_MODEL)),
                  full((1, D_MODEL)), full((1, D_MODEL))],
        out_specs=row(0),
        out_shape=jax.ShapeDtypeStruct((m, D_MODEL), F32),
        compiler_params=_cparams("parallel"),
        name="merge_out_proj",
    )(att2, yl2, z2, z2, h2, wa, wl, wo, bga, bgr)


def _ffn_kernel(*refs, chunks_per_expert):
    if chunks_per_expert:
        h_ref, g_ref, wr_ref, wg_ref, wu_ref, wd_ref, o_ref, hn_ref, acc_ref, gate_ref = refs
    else:
        h_ref, g_ref, wg_ref, wu_ref, wd_ref, o_ref, hn_ref, acc_ref = refs
    j = pl.program_id(1)

    @pl.when(j == 0)
    def _():
        h = h_ref[...]
        hn = _rmsnorm_rows(h, g_ref[...])
        hn_ref[...] = hn.astype(BF16)
        acc_ref[...] = h
        if chunks_per_expert:
            logits = jnp.dot(hn, wr_ref[...], preferred_element_type=F32,
                             precision=lax.Precision.HIGHEST)
            lane = lax.broadcasted_iota(jnp.int32, logits.shape, 1).astype(F32)
            lg = jnp.where(lane < N_EXPERTS, logits, NEG_BIG)
            m1 = jnp.max(lg, axis=-1, keepdims=True)
            i1 = jnp.min(jnp.where(lg == m1, lane, float(LANE)), axis=-1, keepdims=True)
            lg2 = jnp.where(lane == i1, NEG_BIG, lg)
            m2 = jnp.max(lg2, axis=-1, keepdims=True)
            i2 = jnp.min(jnp.where(lg2 == m2, lane, float(LANE)), axis=-1, keepdims=True)
            e2 = jnp.exp(m2 - m1)
            den = 1.0 + e2
            gate_ref[...] = jnp.where(lane == i1, 1.0 / den,
                                      jnp.where(lane == i2, e2 / den, 0.0))

    hn = hn_ref[...]
    gg = jnp.dot(hn, wg_ref[...], preferred_element_type=F32)
    uu = jnp.dot(hn, wu_ref[...], preferred_element_type=F32)
    act = (jax.nn.silu(gg) * uu).astype(BF16)
    y = jnp.dot(act, wd_ref[...], preferred_element_type=F32)
    if chunks_per_expert:
        e = (j // chunks_per_expert).astype(F32)
        gates = gate_ref[...]
        lane = lax.broadcasted_iota(jnp.int32, gates.shape, 1).astype(F32)
        y = y * jnp.sum(jnp.where(lane == e, gates, 0.0), axis=-1, keepdims=True)
    acc_ref[...] += y

    @pl.when(j == pl.num_programs(1) - 1)
    def _():
        o_ref[...] = acc_ref[...]


def _ffn(h2, g, wg, wu, wd, router=None, d_ff_expert=None):
    m = h2.shape[0]
    f = wg.shape[1]
    tm = _largest_tile(m, 1024)
    tf = 512
    in_specs = [pl.BlockSpec((tm, D_MODEL), lambda i, j: (i, 0)),
                pl.BlockSpec((1, D_MODEL), lambda i, j: (0, 0))]
    args = [h2, g]
    scratch = [pltpu.VMEM((tm, D_MODEL), BF16), pltpu.VMEM((tm, D_MODEL), F32)]
    cpe = 0
    if router is not None:
        assert d_ff_expert % tf == 0
        cpe = d_ff_expert // tf
        in_specs.append(pl.BlockSpec((D_MODEL, LANE), lambda i, j: (0, 0)))
        args.append(router)
        scratch.append(pltpu.VMEM((tm, LANE), F32))
    in_specs += [pl.BlockSpec((D_MODEL, tf), lambda i, j: (0, j)),
                 pl.BlockSpec((D_MODEL, tf), lambda i, j: (0, j)),
                 pl.BlockSpec((tf, D_MODEL), lambda i, j: (j, 0))]
    args += [wg, wu, wd]
    return pl.pallas_call(
        functools.partial(_ffn_kernel, chunks_per_expert=cpe),
        grid=(m // tm, f // tf),
        in_specs=in_specs,
        out_specs=pl.BlockSpec((tm, D_MODEL), lambda i, j: (i, 0)),
        out_shape=jax.ShapeDtypeStruct((m, D_MODEL), F32),
        scratch_shapes=scratch,
        compiler_params=_cparams("parallel", "arbitrary"),
        name="moe_ffn" if cpe else "dense_ffn",
    )(*args)


def _blockdiag_groups(w, per):
    g = LRU_BLOCKS // per
    w = w.reshape(g, per, LRU_BLOCK_W, LRU_BLOCK_W)
    eye = jnp.eye(per, dtype=w.dtype)
    out = jnp.einsum('gpio,pq->gpiqo', w, eye)
    return out.reshape(g, per * LRU_BLOCK_W, per * LRU_BLOCK_W)


def _prepare(p):
    depth = p['w_in'].shape[0]
    per = LRU_CB // LRU_BLOCK_W
    layers = []
    slopes = jnp.exp2(-8.0 * jnp.arange(1, N_HEADS + 1, dtype=F32) / N_HEADS)
    for l in range(depth):
        lam_init = 0.8 - 0.6 * math.exp(-0.3 * l)
        wcat = jnp.concatenate(
            [_blockdiag_groups(p['lru_wa'][l, 0], per), _blockdiag_groups(p['lru_wx'][l, 0], per),
             _blockdiag_groups(p['lru_wa'][l, 1], per), _blockdiag_groups(p['lru_wx'][l, 1], per)],
            axis=-1).astype(BF16)
        lay = dict(
            norm_mix=p['norm_mix'][l][None], norm_ffn=p['norm_ffn'][l][None],
            w_in=p['w_in'][l].astype(BF16),
            scal=jnp.concatenate([slopes, jnp.full((N_HEADS,), lam_init, F32)]),
            qg=jnp.tile(p['q_norm'][l], 2)[None], kg=jnp.tile(p['k_norm'][l], 2)[None],
            lamp=jnp.stack([p['lambda_q1'][l], p['lambda_k1'][l],
                            p['lambda_q2'][l], p['lambda_k2'][l]]),
            sg=p['attn_subln'][l][None],
            conv_w=p['conv_w'][l], conv_b=p['conv_b'][l][None], wcat=wcat,
            ba=p['lru_ba'][l], bx=p['lru_bx'][l], lam=p['lru_lambda'][l],
            wa=p['w_attn_branch'][l].astype(BF16), wl=p['w_lru_branch'][l].astype(BF16),
            wo=p['w_out'][l].astype(BF16),
            bga=p['b_gate'][l, :D_MODEL][None], bgr=p['b_gate'][l, D_MODEL:][None],
        )
        j = l // 2
        if l % 2 == 0:
            lay.update(wg=p['ffn_w_gate'][j].astype(BF16), wu=p['ffn_w_up'][j].astype(BF16),
                       wd=p['ffn_w_down'][j].astype(BF16))
        else:
            ne, _, fe = p['moe_w_gate'][j].shape
            lay.update(
                wg=jnp.moveaxis(p['moe_w_gate'][j], 0, 1).reshape(D_MODEL, ne * fe).astype(BF16),
                wu=jnp.moveaxis(p['moe_w_up'][j], 0, 1).reshape(D_MODEL, ne * fe).astype(BF16),
                wd=p['moe_w_down'][j].reshape(ne * fe, D_MODEL).astype(BF16),
                router=jnp.pad(p['moe_router'][j], ((0, 0), (0, LANE - ne))),
                d_ff_expert=fe)
        layers.append(lay)
    return layers


def _trunk(x, meta, layers):
    b, s, _ = x.shape
    seq_len = s + N_META
    lp = -(-seq_len // LANE) * LANE
    h = jnp.concatenate(
        [jnp.broadcast_to(meta[None].astype(x.dtype), (b, N_META, D_MODEL)), x,
         jnp.zeros((b, lp - seq_len, D_MODEL), x.dtype)], axis=1)
    h2 = h.reshape(b * lp, D_MODEL)
    for lay in layers:
        z2 = _in_proj(h2, lay['norm_mix'], lay['w_in'])
        z3 = z2.reshape(b, lp, IN_WIDTH)
        att = _attention(z3, lay['scal'], lay['qg'], lay['kg'], lay['lamp'], lay['sg'], seq_len)
        yl = _lru(z3, lay['conv_w'], lay['conv_b'], lay['wcat'], lay['ba'], lay['bx'],
                  lay['lam'], seq_len)
        h2 = _merge(att.reshape(b * lp, D_MODEL), yl.reshape(b * lp, D_MODEL), z2, h2,
                    lay['wa'], lay['wl'], lay['wo'], lay['bga'], lay['bgr'])
        h2 = _ffn(h2, lay['norm_ffn'], lay['wg'], lay['wu'], lay['wd'],
                  lay.get('router'), lay.get('d_ff_expert'))
    return h2.reshape(b, lp, D_MODEL)[:, N_META:seq_len]


def kernel(x_prompt, x_sample, meta_tokens, norm_mix, norm_ffn, w_in, b_gate, q_norm, k_norm, lambda_q1, lambda_k1, lambda_q2, lambda_k2, attn_subln, w_attn_branch, conv_w, conv_b, lru_wa, lru_ba, lru_wx, lru_bx, lru_lambda, w_lru_branch, w_out, ffn_w_gate, ffn_w_up, ffn_w_down, moe_router, moe_w_gate, moe_w_up, moe_w_down):
    p = dict(norm_mix=norm_mix, norm_ffn=norm_ffn, w_in=w_in, b_gate=b_gate, q_norm=q_norm,
             k_norm=k_norm, lambda_q1=lambda_q1, lambda_k1=lambda_k1, lambda_q2=lambda_q2,
             lambda_k2=lambda_k2, attn_subln=attn_subln, w_attn_branch=w_attn_branch,
             conv_w=conv_w, conv_b=conv_b, lru_wa=lru_wa, lru_ba=lru_ba, lru_wx=lru_wx,
             lru_bx=lru_bx, lru_lambda=lru_lambda, w_lru_branch=w_lru_branch, w_out=w_out,
             ffn_w_gate=ffn_w_gate, ffn_w_up=ffn_w_up, ffn_w_down=ffn_w_down,
             moe_router=moe_router, moe_w_gate=moe_w_gate, moe_w_up=moe_w_up,
             moe_w_down=moe_w_down)
    layers = _prepare(p)
    return (_trunk(x_prompt, meta_tokens, layers), _trunk(x_sample, meta_tokens, layers))
```

```python
import functools
import math

import jax
import jax.numpy as jnp
from jax import lax
from jax.experimental import pallas as pl
from jax.experimental.pallas import tpu as pltpu

F32 = jnp.float32
BF16 = jnp.bfloat16

D_MODEL = 1024
N_META = 16
N_HEADS = 8
HEAD_DIM = 64
V_DIM = 2 * HEAD_DIM
LRU_BLOCKS = 16
LRU_BLOCK_W = D_MODEL // LRU_BLOCKS
LRU_C = 8.0
N_EXPERTS = 8
EPS = 1e-6
IN_WIDTH = 7 * D_MODEL

LANE = 128
SUBLANE = 8
VMEM_LIMIT_BYTES = 48 * 1024 * 1024
NEG_BIG = -1e30

ATT_TILE = 512
ATT_MAX_UNROLL = 8
EXP_ZERO_LOGIT = 105.0
VT_ROWS = V_DIM + 16
ATT_TAIL = LANE
ATT_VMEM_LIMIT_BYTES = 56 * 1024 * 1024
MASK_LOGIT = -30000.0
SAFE_LOGIT_SPAN = 80.0

LRU_CB = 128
LRU_TT = 128


def _largest_tile(n, target, mult=LANE):
    assert n % mult == 0, (n, mult)
    best = mult
    t = mult
    while t <= min(n, target):
        if n % t == 0:
            best = t
        t += mult
    return best


def _cparams(*sem):
    return pltpu.CompilerParams(dimension_semantics=sem, vmem_limit_bytes=VMEM_LIMIT_BYTES)


def _rmsnorm_rows(x, g):
    ms = jnp.mean(x * x, axis=-1, keepdims=True)
    return x * lax.rsqrt(ms + EPS) * g


def _in_proj_kernel(h_ref, g_ref, w_ref, z_ref, xn_ref):
    @pl.when(pl.program_id(1) == 0)
    def _():
        xn_ref[...] = _rmsnorm_rows(h_ref[...], g_ref[...]).astype(BF16)

    z_ref[...] = jnp.dot(xn_ref[...], w_ref[...], preferred_element_type=F32)


def _in_proj(h2, g, w):
    m = h2.shape[0]
    tm = _largest_tile(m, 1024)
    tn = _largest_tile(IN_WIDTH, 1792, 2 * LANE)
    return pl.pallas_call(
        _in_proj_kernel,
        grid=(m // tm, IN_WIDTH // tn),
        in_specs=[
            pl.BlockSpec((tm, D_MODEL), lambda i, j: (i, 0)),
            pl.BlockSpec((1, D_MODEL), lambda i, j: (0, 0)),
            pl.BlockSpec((D_MODEL, tn), lambda i, j: (0, j)),
        ],
        out_specs=pl.BlockSpec((tm, tn), lambda i, j: (i, j)),
        out_shape=jax.ShapeDtypeStruct((m, IN_WIDTH), F32),
        scratch_shapes=[pltpu.VMEM((tm, D_MODEL), BF16)],
        compiler_params=_cparams("parallel", "arbitrary"),
        name="in_proj",
    )(h2, g, w)


def _attn_kernel(sc_ref, q_ref, k_ref, v_ref, qg_ref, kg_ref, lam_ref, sg_ref, o_ref,
                 kb_ref, vb_ref, m_ref, l_ref, acc_ref, *, seq_len, tq, tk):
    lp = k_ref.shape[0]
    nk = lp // tk
    head = pl.program_id(1)
    qi = pl.program_id(2)
    lane = lax.broadcasted_iota(jnp.int32, (1, LANE), 1)
    lo = lane < HEAD_DIM

    def half_rmsnorm(x, g):
        sq = x * x
        s_lo = jnp.sum(jnp.where(lo, sq, 0.0), axis=-1, keepdims=True)
        s_hi = jnp.sum(jnp.where(lo, 0.0, sq), axis=-1, keepdims=True)
        ms = jnp.where(lo, s_lo, s_hi) * (1.0 / HEAD_DIM)
        return x * lax.rsqrt(ms + EPS) * g

    @pl.when(qi == 0)
    def _():
        def fill(c, carry):
            r = pl.ds(pl.multiple_of(c * tk, tk), tk)
            kb_ref[r, :] = half_rmsnorm(k_ref[r, :], kg_ref[...]).astype(BF16)
            vb_ref[r, :] = v_ref[r, :].astype(BF16)
            return carry
        lax.fori_loop(0, nk, fill, 0)

    qn = half_rmsnorm(q_ref[...], qg_ref[...]) * (1.0 / math.sqrt(HEAD_DIM))
    q_comp = (jnp.where(lo, qn, 0.0).astype(BF16), jnp.where(lo, 0.0, qn).astype(BF16))

    m_ref[...] = jnp.full(m_ref.shape, NEG_BIG, F32)
    l_ref[...] = jnp.zeros(l_ref.shape, F32)
    acc_ref[...] = jnp.zeros(acc_ref.shape, F32)

    slope = sc_ref[head]
    rel = (lax.broadcasted_iota(jnp.int32, (tq, tk), 0)
           - lax.broadcasted_iota(jnp.int32, (tq, tk), 1)).astype(F32)
    col = lax.broadcasted_iota(jnp.int32, (1, tk), 1)

    def chunk(kc, carry):
        k0 = pl.multiple_of(kc * tk, tk)
        kblk = kb_ref[pl.ds(k0, tk), :]
        vblk = vb_ref[pl.ds(k0, tk), :]
        off = (qi * tq - k0).astype(F32)
        bias = -slope * jnp.abs(rel + off) + jnp.where(col + k0 < seq_len, 0.0, NEG_BIG)
        for c in range(2):
            s = lax.dot_general(q_comp[c], kblk, (((1,), (1,)), ((), ())),
                                preferred_element_type=F32) + bias
            m_prev = m_ref[c]
            m_new = jnp.maximum(m_prev, jnp.max(s, axis=-1, keepdims=True))
            alpha = jnp.exp(m_prev - m_new)
            p = jnp.exp(s - m_new)
            l_ref[c] = alpha * l_ref[c] + jnp.sum(p, axis=-1, keepdims=True)
            acc_ref[c] = alpha * acc_ref[c] + jnp.dot(p.astype(BF16), vblk,
                                                      preferred_element_type=F32)
            m_ref[c] = m_new
        return carry

    lax.fori_loop(0, nk, chunk, 0)

    lam_init = sc_ref[N_HEADS]
    lp_ = lam_ref[...]
    lam = (jnp.exp(jnp.sum(lp_[0:1] * lp_[1:2], axis=-1, keepdims=True))
           - jnp.exp(jnp.sum(lp_[2:3] * lp_[3:4], axis=-1, keepdims=True)) + lam_init)
    o = acc_ref[0] / l_ref[0] - lam * (acc_ref[1] / l_ref[1])
    o = _rmsnorm_rows(o, sg_ref[...]) * (1.0 - lam_init)
    o_ref[...] = o.astype(BF16)


def _attn_bounded_kernel(sc_ref, q_ref, k_ref, v_ref, qg_ref, kg_ref, lam_ref, sg_ref, prev_ref,
                         o_ref, kq_ref, vt_ref, vtail_ref, lhs_ref, bias_ref, acc_ref,
                         *, seq_len, t, head0, reach):
    lp = k_ref.shape[0]
    s_main = lp - ATT_TAIL
    n_main = s_main // t
    del prev_ref
    slope = sc_ref[head0 + pl.program_id(1)]
    lam_init = sc_ref[N_HEADS]
    lane = lax.broadcasted_iota(jnp.int32, (1, LANE), 1)
    lo = lane < HEAD_DIM
    nt = (((1,), (1,)), ((), ()))

    def half_rmsnorm(x, g):
        sq = x * x
        s_lo = jnp.sum(jnp.where(lo, sq, 0.0), axis=-1, keepdims=True)
        s_hi = jnp.sum(jnp.where(lo, 0.0, sq), axis=-1, keepdims=True)
        ms = jnp.where(lo, s_lo, s_hi) * (1.0 / HEAD_DIM)
        return x * lax.rsqrt(ms + EPS) * g

    def split2(x):
        hi = x.astype(BF16).astype(F32)
        return hi, x - hi

    def pos_col(r0, n):
        return lax.broadcasted_iota(jnp.int32, (n, 1), 0) + r0

    lp_ = lam_ref[...]
    lam = (jnp.exp(jnp.sum(lp_[0:1] * lp_[1:2], axis=-1, keepdims=True))
           - jnp.exp(jnp.sum(lp_[2:3] * lp_[3:4], axis=-1, keepdims=True)) + lam_init)

    ones_row = jnp.where(lax.broadcasted_iota(jnp.int32, (VT_ROWS - V_DIM, 1), 0) == 0, 1.0, 0.0)

    def fill(r0, n, kmax, vt_dst):
        rows = pos_col(r0, n)
        valid = rows < seq_len
        kn = jnp.where(valid, half_rmsnorm(k_ref[pl.ds(r0, n), :], kg_ref[...]), 0.0)
        sq = kn * kn
        n_lo = jnp.max(jnp.sum(jnp.where(lo, sq, 0.0), axis=-1, keepdims=True), axis=0, keepdims=True)
        n_hi = jnp.max(jnp.sum(jnp.where(lo, 0.0, sq), axis=-1, keepdims=True), axis=0, keepdims=True)
        sj_hi, sj_lo = split2(slope * rows.astype(F32))
        aug = jnp.where(lane == 0, -1.0,
              jnp.where((lane == 1) | (lane == 2), 1.0,
              jnp.where(lane == 3, sj_hi,
              jnp.where(lane == 4, sj_lo,
              jnp.where(lane == 5, jnp.where(valid, 0.0, MASK_LOGIT), 0.0)))))
        kq_ref[pl.ds(r0, n), 0:LANE] = kn.astype(BF16)
        kq_ref[pl.ds(r0, n), LANE:2 * LANE] = aug.astype(BF16)
        vt = jnp.transpose(jnp.where(valid, v_ref[pl.ds(r0, n), :], 0.0))
        vt_dst[...] = jnp.concatenate(
            [vt, jnp.broadcast_to(ones_row, (VT_ROWS - V_DIM, n))], axis=0).astype(BF16)
        return jnp.maximum(kmax[0], n_lo), jnp.maximum(kmax[1], n_hi)

    zero11 = jnp.zeros((1, 1), F32)
    kmax = lax.fori_loop(
        0, n_main, lambda c, km: fill(pl.multiple_of(c * t, t), t, km, vt_ref.at[c]),
        (zero11, zero11))
    kmax = fill(s_main, ATT_TAIL, kmax, vtail_ref)
    knorm = (jnp.sqrt(kmax[0]), jnp.sqrt(kmax[1]))

    rel = (lax.broadcasted_iota(jnp.int32, (t, t), 0)
           - lax.broadcasted_iota(jnp.int32, (t, t), 1)).astype(F32)
    bias_ref[0] = jnp.zeros((t, t), F32)
    bias_ref[1] = -slope * jnp.abs(rel)

    def build_lhs(r0, n):
        rows = pos_col(r0, n)
        qn = half_rmsnorm(q_ref[pl.ds(r0, n), :], qg_ref[...]) * (1.0 / math.sqrt(HEAD_DIM))
        si_hi, si_lo = split2(slope * rows.astype(F32))
        side = jnp.where(lane == 1, si_hi,
               jnp.where(lane == 2, si_lo,
               jnp.where((lane == 3) | (lane == 4), -1.0, 0.0)))
        for c in range(2):
            qc = jnp.where(lo, qn, 0.0) if c == 0 else jnp.where(lo, 0.0, qn)
            bound = jnp.sqrt(jnp.sum(qc * qc, axis=-1, keepdims=True)) * knorm[c]
            base = jnp.where(lane == 0, bound, jnp.where(lane == 5, 1.0, 0.0))
            qcb = qc.astype(BF16)
            for x, aug in enumerate((base - side, base, base + side)):
                lhs_ref[x, c * n:(c + 1) * n, 0:LANE] = qcb
                lhs_ref[x, c * n:(c + 1) * n, LANE:2 * LANE] = aug.astype(BF16)

    def process(x, n, k0, nk, vt, bias):
        s = lax.dot_general(kq_ref[pl.ds(k0, nk), :], lhs_ref[x, 0:2 * n, :], nt,
                            preferred_element_type=F32)
        if bias is not None:
            s = s + jnp.concatenate([bias, bias], axis=1)
        p = jnp.exp(s).astype(BF16)
        acc_ref[:, 0:2 * n] += jnp.dot(vt, p, preferred_element_type=F32)

    def finalize(r0, n):
        a0 = acc_ref[:, 0:n]
        a1 = acc_ref[:, n:2 * n]
        o = (a0[0:V_DIM] * (1.0 / a0[V_DIM:V_DIM + 1])
             - (lam * (1.0 / a1[V_DIM:V_DIM + 1])) * a1[0:V_DIM])
        ms = jnp.mean(o * o, axis=0, keepdims=True)
        o = o * lax.rsqrt(ms + EPS) * sg_ref[...] * (1.0 - lam_init)
        pos = lax.broadcasted_iota(jnp.int32, (1, n), 1) + r0
        o = jnp.where(pos < seq_len, o, 0.0)
        o_ref[pl.ds(r0, n), :] = jnp.transpose(o).astype(BF16)

    window = min(2 * reach + 1, n_main)

    def full_chunks(qi, n, kc0, count):
        def body(w, carry):
            kc = kc0 + w
            x = jnp.where(kc < qi, 0, jnp.where(kc == qi, 1, 2))
            d = jnp.where(kc == qi, 1, 0)
            process(x, n, pl.multiple_of(kc * t, t), t, vt_ref[kc], bias_ref[d, :, 0:n])
            return carry
        lax.fori_loop(0, count, body, 0, unroll=min(count, ATT_MAX_UNROLL))

    def q_main(qi, carry):
        r0 = pl.multiple_of(qi * t, t)
        build_lhs(r0, t)
        acc_ref[...] = jnp.zeros(acc_ref.shape, F32)
        full_chunks(qi, t, jnp.clip(qi - reach, 0, n_main - window), window)

        @pl.when(n_main - qi <= reach)
        def _():
            process(2, t, s_main, ATT_TAIL, vtail_ref[...], None)
        finalize(r0, t)
        return carry
    lax.fori_loop(0, n_main, q_main, 0)

    build_lhs(s_main, ATT_TAIL)
    acc_ref[...] = jnp.zeros(acc_ref.shape, F32)
    tail_count = min(reach, n_main)
    full_chunks(n_main, ATT_TAIL, n_main - tail_count, tail_count)
    process(1, ATT_TAIL, s_main, ATT_TAIL, vtail_ref[...], bias_ref[1, 0:ATT_TAIL, 0:ATT_TAIL])
    finalize(s_main, ATT_TAIL)


def _chunk_reach(head, t):
    slope = 2.0 ** (-8.0 * (head + 1) / N_HEADS)
    return max(1, math.ceil((EXP_ZERO_LOGIT / slope - 1.0) / t))


def _attention_bounded(z3, scal, qg, kg, lamp, sg, seq_len):
    b, lp, _ = z3.shape
    t = _largest_tile(lp - ATT_TAIL, ATT_TILE)
    n_main = (lp - ATT_TAIL) // t
    groups = []
    for head in range(N_HEADS):
        reach = min(_chunk_reach(head, t), n_main)
        if groups and min(2 * groups[-1][2] + 1, n_main) == min(2 * reach + 1, n_main):
            groups[-1][1] += 1
            groups[-1][2] = max(groups[-1][2], reach)
        else:
            groups.append([head, 1, reach])
    small = lambda shape: pl.BlockSpec(shape, lambda bi, hi: (0, 0))
    att = jnp.zeros((b, lp, N_HEADS * V_DIM), BF16)
    for head0, count, reach in groups:
        kern = functools.partial(_attn_bounded_kernel, seq_len=seq_len, t=t, head0=head0,
                                 reach=reach)
        col = lambda off, head0=head0: pl.BlockSpec(
            (None, lp, V_DIM), lambda bi, hi: (bi, 0, off + head0 + hi))
        att = pl.pallas_call(
            kern,
            grid=(b, count),
            in_specs=[
                pl.BlockSpec(memory_space=pltpu.SMEM),
                col(0), col(N_HEADS), col(2 * N_HEADS),
                small((1, V_DIM)), small((1, V_DIM)), small((4, HEAD_DIM)), small((V_DIM, 1)),
                pl.BlockSpec(memory_space=pl.ANY),
            ],
            out_specs=col(0),
            out_shape=jax.ShapeDtypeStruct(att.shape, att.dtype),
            input_output_aliases={8: 0},
            scratch_shapes=[
                pltpu.VMEM((lp, 2 * LANE), BF16),
                pltpu.VMEM((n_main, VT_ROWS, t), BF16), pltpu.VMEM((VT_ROWS, ATT_TAIL), BF16),
                pltpu.VMEM((3, 2 * t, 2 * LANE), BF16),
                pltpu.VMEM((2, t, t), F32),
                pltpu.VMEM((VT_ROWS, 2 * t), F32),
            ],
            compiler_params=pltpu.CompilerParams(dimension_semantics=("parallel", "parallel"),
                                                 vmem_limit_bytes=ATT_VMEM_LIMIT_BYTES),
            name="diff_attention_bounded",
        )(scal, z3, z3, z3, qg, kg, lamp, sg.reshape(V_DIM, 1), att)
    return att


def _attention(z3, scal, qg, kg, lamp, sg, seq_len):
    span = 2.0 * HEAD_DIM / math.sqrt(HEAD_DIM) * jnp.max(jnp.abs(qg)) * jnp.max(jnp.abs(kg))
    args = (z3, scal, qg, kg, lamp, sg)
    return lax.cond(span <= SAFE_LOGIT_SPAN,
                    lambda *a: _attention_bounded(*a, seq_len),
                    lambda *a: _attention_online(*a, seq_len), *args)


def _attention_online(z3, scal, qg, kg, lamp, sg, seq_len):
    b, lp, _ = z3.shape
    tq = _largest_tile(lp, 768)
    tk = tq
    kern = functools.partial(_attn_kernel, seq_len=seq_len, tq=tq, tk=tk)
    small = lambda shape: pl.BlockSpec(shape, lambda bi, hi, qi: (0, 0))
    return pl.pallas_call(
        kern,
        grid=(b, N_HEADS, lp // tq),
        in_specs=[
            pl.BlockSpec(memory_space=pltpu.SMEM),
            pl.BlockSpec((None, tq, V_DIM), lambda bi, hi, qi: (bi, qi, hi)),
            pl.BlockSpec((None, lp, V_DIM), lambda bi, hi, qi: (bi, 0, N_HEADS + hi)),
            pl.BlockSpec((None, lp, V_DIM), lambda bi, hi, qi: (bi, 0, 2 * N_HEADS + hi)),
            small((1, V_DIM)), small((1, V_DIM)), small((4, HEAD_DIM)), small((1, V_DIM)),
        ],
        out_specs=pl.BlockSpec((None, tq, V_DIM), lambda bi, hi, qi: (bi, qi, hi)),
        out_shape=jax.ShapeDtypeStruct((b, lp, N_HEADS * V_DIM), BF16),
        scratch_shapes=[
            pltpu.VMEM((lp, V_DIM), BF16), pltpu.VMEM((lp, V_DIM), BF16),
            pltpu.VMEM((2, tq, 1), F32), pltpu.VMEM((2, tq, 1), F32),
            pltpu.VMEM((2, tq, V_DIM), F32),
        ],
        compiler_params=_cparams("parallel", "parallel", "arbitrary"),
        name="diff_attention",
    )(scal, z3, z3, z3, qg, kg, lamp, sg)


def _lru_kernel(xl_ref, gl_ref, cw_ref, cbias_ref, w_ref, ba_ref, bx_ref, lam_ref, o_ref,
                xp_ref, hf_ref, hb_ref, *, seq_len, tt):
    lp, cb = xl_ref.shape
    n = lp // tt
    pad = 8
    rows_t = lax.broadcasted_iota(jnp.int32, (tt, 1), 0)

    xp_ref[0:pad, :] = jnp.zeros((pad, cb), F32)
    xp_ref[lp + pad:lp + 2 * pad, :] = jnp.zeros((pad, cb), F32)

    def copy_in(c, carry):
        r0 = pl.multiple_of(c * tt, tt)
        x = xl_ref[pl.ds(r0, tt), :]
        xp_ref[pl.ds(r0 + pad, tt), :] = jnp.where(rows_t + r0 < seq_len, x, 0.0)
        return carry
    lax.fori_loop(0, n, copy_in, 0)

    cw = cw_ref[...]

    def gates(c, d):
        r0 = pl.multiple_of(c * tt, tt)
        win = xp_ref[pl.ds(r0, tt + 2 * pad), :]
        wn = tt + 2 * pad
        xc = (pltpu.roll(win, 2, 0)[pad:pad + tt] * cw[0:1]
              + pltpu.roll(win, 1, 0)[pad:pad + tt] * cw[1:2]
              + win[pad:pad + tt] * cw[2:3]
              + pltpu.roll(win, wn - 1, 0)[pad:pad + tt] * cw[3:4]
              + cbias_ref[...])
        pre = jnp.dot(xc.astype(BF16), w_ref[:, d * 2 * cb:(d + 1) * 2 * cb],
                      preferred_element_type=F32)
        r = jax.nn.sigmoid(pre[:, :cb] + ba_ref[d:d + 1, :])
        i = jax.nn.sigmoid(pre[:, cb:] + bx_ref[d:d + 1, :])
        nl = -lam_ref[d:d + 1, :]
        softplus = jnp.maximum(nl, 0.0) + jnp.log1p(jnp.exp(-jnp.abs(nl)))
        log_a = (-LRU_C * r) * softplus
        a = jnp.exp(log_a)
        th = jnp.tanh(log_a)
        mult = jnp.sqrt(-2.0 * th / (1.0 - th))
        u = mult * (i * xc)
        u = jnp.where(rows_t + r0 < seq_len, u, 0.0)
        return a, u

    ng = tt // SUBLANE
    sub = lax.broadcasted_iota(jnp.int32, (1, SUBLANE, 1), 1)

    def scan_chunk(a, u, carry, reverse):
        a3 = a.reshape(ng, SUBLANE, cb)
        u3 = u.reshape(ng, SUBLANE, cb)
        d = 1
        while d < SUBLANE:
            if reverse:
                shift, keep = SUBLANE - d, sub < SUBLANE - d
            else:
                shift, keep = d, sub >= d
            u3 = u3 + a3 * jnp.where(keep, pltpu.roll(u3, shift, 1), 0.0)
            a3 = a3 * jnp.where(keep, pltpu.roll(a3, shift, 1), 1.0)
            d *= 2
        hs = [None] * ng
        for g in (range(ng - 1, -1, -1) if reverse else range(ng)):
            hg = u3[g] + a3[g] * carry
            hs[g] = hg
            carry = hg[0:1, :] if reverse else hg[SUBLANE - 1:SUBLANE, :]
        return jnp.concatenate(hs, axis=0), carry

    def both(j, carry):
        cf = j
        h, state_f = scan_chunk(*gates(cf, 0), carry[0], False)
        hf_ref[pl.ds(pl.multiple_of(cf * tt, tt), tt), :] = h
        cr = n - 1 - j
        h, state_b = scan_chunk(*gates(cr, 1), carry[1], True)
        hb_ref[pl.ds(pl.multiple_of(cr * tt, tt), tt), :] = h
        return state_f, state_b
    zero_state = jnp.zeros((1, cb), F32)
    lax.fori_loop(0, n, both, (zero_state, zero_state))

    def combine(c, carry):
        r = pl.ds(pl.multiple_of(c * tt, tt), tt)
        o_ref[r, :] = ((hf_ref[r, :] + hb_ref[r, :]) * jax.nn.gelu(gl_ref[r, :])).astype(BF16)
        return carry
    lax.fori_loop(0, n, combine, 0)


def _lru(z3, cw, cbias, wcat, ba, bx, lam, seq_len):
    b, lp, _ = z3.shape
    cb = LRU_CB
    nc = D_MODEL // cb
    kern = functools.partial(_lru_kernel, seq_len=seq_len, tt=LRU_TT)
    chan = lambda rows: pl.BlockSpec((rows, cb), lambda bi, ci: (0, ci))
    return pl.pallas_call(
        kern,
        grid=(b, nc),
        in_specs=[
            pl.BlockSpec((None, lp, cb), lambda bi, ci: (bi, 0, 3 * nc + ci)),
            pl.BlockSpec((None, lp, cb), lambda bi, ci: (bi, 0, 4 * nc + ci)),
            chan(4), chan(1),
            pl.BlockSpec((None, cb, 4 * cb), lambda bi, ci: (ci, 0, 0)),
            chan(2), chan(2), chan(2),
        ],
        out_specs=pl.BlockSpec((None, lp, cb), lambda bi, ci: (bi, 0, ci)),
        out_shape=jax.ShapeDtypeStruct((b, lp, D_MODEL), BF16),
        scratch_shapes=[pltpu.VMEM((lp + 16, cb), F32), pltpu.VMEM((lp, cb), F32),
                        pltpu.VMEM((lp, cb), F32)],
        compiler_params=_cparams("parallel", "parallel"),
        name="rglru",
    )(z3, z3, cw, cbias, wcat, ba, bx, lam)


def _merge_kernel(att_ref, yl_ref, ga_ref, gr_ref, h_ref, wa_ref, wl_ref, wo_ref,
                  bga_ref, bgr_ref, o_ref):
    ya = jnp.dot(att_ref[...], wa_ref[...], preferred_element_type=F32)
    yr = jnp.dot(yl_ref[...], wl_ref[...], preferred_element_type=F32)
    merged = (jax.nn.sigmoid(ga_ref[...] + bga_ref[...]) * ya
              + jax.nn.sigmoid(gr_ref[...] + bgr_ref[...]) * yr)
    o_ref[...] = h_ref[...] + jnp.dot(merged.astype(BF16), wo_ref[...],
                                      preferred_element_type=F32)


def _merge(att2, yl2, z2, h2, wa, wl, wo, bga, bgr):
    m = h2.shape[0]
    tm = _largest_tile(m, 256)
    row = lambda col: pl.BlockSpec((tm, D_MODEL), lambda i: (i, col))
    full = lambda shape: pl.BlockSpec(shape, lambda i: (0, 0))
    return pl.pallas_call(
        _merge_kernel,
        grid=(m // tm,),
        in_specs=[row(0), row(0), row(5), row(6), row(0),
                  full((D_MODEL, D_MODEL)), full((D_MODEL, D_MODEL)), full((D_MODEL, D_MODEL)),
                  full((1, D_MODEL)), full((1, D_MODEL))],
        out_specs=row(0),
        out_shape=jax.ShapeDtypeStruct((m, D_MODEL), F32),
        compiler_params=_cparams("parallel"),
        name="merge_out_proj",
    )(att2, yl2, z2, z2, h2, wa, wl, wo, bga, bgr)


def _ffn_kernel(*refs, chunks_per_expert):
    if chunks_per_expert:
        h_ref, g_ref, wr_ref, wg_ref, wu_ref, wd_ref, o_ref, hn_ref, acc_ref, gate_ref = refs
    else:
        h_ref, g_ref, wg_ref, wu_ref, wd_ref, o_ref, hn_ref, acc_ref = refs
    j = pl.program_id(1)

    @pl.when(j == 0)
    def _():
        h = h_ref[...]
        hn = _rmsnorm_rows(h, g_ref[...])
        hn_ref[...] = hn.astype(BF16)
        acc_ref[...] = h
        if chunks_per_expert:
            logits = jnp.dot(hn, wr_ref[...], preferred_element_type=F32,
                             precision=lax.Precision.HIGHEST)
            lane = lax.broadcasted_iota(jnp.int32, logits.shape, 1).astype(F32)
            lg = jnp.where(lane < N_EXPERTS, logits, NEG_BIG)
            m1 = jnp.max(lg, axis=-1, keepdims=True)
            i1 = jnp.min(jnp.where(lg == m1, lane, float(LANE)), axis=-1, keepdims=True)
            lg2 = jnp.where(lane == i1, NEG_BIG, lg)
            m2 = jnp.max(lg2, axis=-1, keepdims=True)
            i2 = jnp.min(jnp.where(lg2 == m2, lane, float(LANE)), axis=-1, keepdims=True)
            e2 = jnp.exp(m2 - m1)
            den = 1.0 + e2
            gate_ref[...] = jnp.where(lane == i1, 1.0 / den,
                                      jnp.where(lane == i2, e2 / den, 0.0))

    hn = hn_ref[...]
    gg = jnp.dot(hn, wg_ref[...], preferred_element_type=F32)
    uu = jnp.dot(hn, wu_ref[...], preferred_element_type=F32)
    act = (jax.nn.silu(gg) * uu).astype(BF16)
    y = jnp.dot(act, wd_ref[...], preferred_element_type=F32)
    if chunks_per_expert:
        e = (j // chunks_per_expert).astype(F32)
        gates = gate_ref[...]
        lane = lax.broadcasted_iota(jnp.int32, gates.shape, 1).astype(F32)
        y = y * jnp.sum(jnp.where(lane == e, gates, 0.0), axis=-1, keepdims=True)
    acc_ref[...] += y

    @pl.when(j == pl.num_programs(1) - 1)
    def _():
        o_ref[...] = acc_ref[...]


def _ffn(h2, g, wg, wu, wd, router=None, d_ff_expert=None):
    m = h2.shape[0]
    f = wg.shape[1]
    tm = _largest_tile(m, 1024)
    tf = 512
    in_specs = [pl.BlockSpec((tm, D_MODEL), lambda i, j: (i, 0)),
                pl.BlockSpec((1, D_MODEL), lambda i, j: (0, 0))]
    args = [h2, g]
    scratch = [pltpu.VMEM((tm, D_MODEL), BF16), pltpu.VMEM((tm, D_MODEL), F32)]
    cpe = 0
    if router is not None:
        assert d_ff_expert % tf == 0
        cpe = d_ff_expert // tf
        in_specs.append(pl.BlockSpec((D_MODEL, LANE), lambda i, j: (0, 0)))
        args.append(router)
        scratch.append(pltpu.VMEM((tm, LANE), F32))
    in_specs += [pl.BlockSpec((D_MODEL, tf), lambda i, j: (0, j)),
                 pl.BlockSpec((D_MODEL, tf), lambda i, j: (0, j)),
                 pl.BlockSpec((tf, D_MODEL), lambda i, j: (j, 0))]
    args += [wg, wu, wd]
    return pl.pallas_call(
        functools.partial(_ffn_kernel, chunks_per_expert=cpe),
        grid=(m // tm, f // tf),
        in_specs=in_specs,
        out_specs=pl.BlockSpec((tm, D_MODEL), lambda i, j: (i, 0)),
        out_shape=jax.ShapeDtypeStruct((m, D_MODEL), F32),
        scratch_shapes=scratch,
        compiler_params=_cparams("parallel", "arbitrary"),
        name="moe_ffn" if cpe else "dense_ffn",
    )(*args)


def _blockdiag_groups(w, per):
    g = LRU_BLOCKS // per
    w = w.reshape(g, per, LRU_BLOCK_W, LRU_BLOCK_W)
    eye = jnp.eye(per, dtype=w.dtype)
    out = jnp.einsum('gpio,pq->gpiqo', w, eye)
    return out.reshape(g, per * LRU_BLOCK_W, per * LRU_BLOCK_W)


def _prepare(p):
    depth = p['w_in'].shape[0]
    per = LRU_CB // LRU_BLOCK_W
    layers = []
    slopes = jnp.exp2(-8.0 * jnp.arange(1, N_HEADS + 1, dtype=F32) / N_HEADS)
    for l in range(depth):
        lam_init = 0.8 - 0.6 * math.exp(-0.3 * l)
        wcat = jnp.concatenate(
            [_blockdiag_groups(p['lru_wa'][l, 0], per), _blockdiag_groups(p['lru_wx'][l, 0], per),
             _blockdiag_groups(p['lru_wa'][l, 1], per), _blockdiag_groups(p['lru_wx'][l, 1], per)],
            axis=-1).astype(BF16)
        lay = dict(
            norm_mix=p['norm_mix'][l][None], norm_ffn=p['norm_ffn'][l][None],
            w_in=p['w_in'][l].astype(BF16),
            scal=jnp.concatenate([slopes, jnp.full((N_HEADS,), lam_init, F32)]),
            qg=jnp.tile(p['q_norm'][l], 2)[None], kg=jnp.tile(p['k_norm'][l], 2)[None],
            lamp=jnp.stack([p['lambda_q1'][l], p['lambda_k1'][l],
                            p['lambda_q2'][l], p['lambda_k2'][l]]),
            sg=p['attn_subln'][l][None],
            conv_w=p['conv_w'][l], conv_b=p['conv_b'][l][None], wcat=wcat,
            ba=p['lru_ba'][l], bx=p['lru_bx'][l], lam=p['lru_lambda'][l],
            wa=p['w_attn_branch'][l].astype(BF16), wl=p['w_lru_branch'][l].astype(BF16),
            wo=p['w_out'][l].astype(BF16),
            bga=p['b_gate'][l, :D_MODEL][None], bgr=p['b_gate'][l, D_MODEL:][None],
        )
        j = l // 2
        if l % 2 == 0:
            lay.update(wg=p['ffn_w_gate'][j].astype(BF16), wu=p['ffn_w_up'][j].astype(BF16),
                       wd=p['ffn_w_down'][j].astype(BF16))
        else:
            ne, _, fe = p['moe_w_gate'][j].shape
            lay.update(
                wg=jnp.moveaxis(p['moe_w_gate'][j], 0, 1).reshape(D_MODEL, ne * fe).astype(BF16),
                wu=jnp.moveaxis(p['moe_w_up'][j], 0, 1).reshape(D_MODEL, ne * fe).astype(BF16),
                wd=p['moe_w_down'][j].reshape(ne * fe, D_MODEL).astype(BF16),
                router=jnp.pad(p['moe_router'][j], ((0, 0), (0, LANE - ne))),
                d_ff_expert=fe)
        layers.append(lay)
    return layers


def _trunk(x, meta, layers):
    b, s, _ = x.shape
    seq_len = s + N_META
    lp = -(-seq_len // LANE) * LANE
    h = jnp.concatenate(
        [jnp.broadcast_to(meta[None].astype(x.dtype), (b, N_META, D_MODEL)), x,
         jnp.zeros((b, lp - seq_len, D_MODEL), x.dtype)], axis=1)
    h2 = h.reshape(b * lp, D_MODEL)
    for lay in layers:
        z2 = _in_proj(h2, lay['norm_mix'], lay['w_in'])
        z3 = z2.reshape(b, lp, IN_WIDTH)
        att = _attention(z3, lay['scal'], lay['qg'], lay['kg'], lay['lamp'], lay['sg'], seq_len)
        yl = _lru(z3, lay['conv_w'], lay['conv_b'], lay['wcat'], lay['ba'], lay['bx'],
                  lay['lam'], seq_len)
        h2 = _merge(att.reshape(b * lp, D_MODEL), yl.reshape(b * lp, D_MODEL), z2, h2,
                    lay['wa'], lay['wl'], lay['wo'], lay['bga'], lay['bgr'])
        h2 = _ffn(h2, lay['norm_ffn'], lay['wg'], lay['wu'], lay['wd'],
                  lay.get('router'), lay.get('d_ff_expert'))
    return h2.reshape(b, lp, D_MODEL)[:, N_META:seq_len]


def kernel(x_prompt, x_sample, meta_tokens, norm_mix, norm_ffn, w_in, b_gate, q_norm, k_norm, lambda_q1, lambda_k1, lambda_q2, lambda_k2, attn_subln, w_attn_branch, conv_w, conv_b, lru_wa, lru_ba, lru_wx, lru_bx, lru_lambda, w_lru_branch, w_out, ffn_w_gate, ffn_w_up, ffn_w_down, moe_router, moe_w_gate, moe_w_up, moe_w_down):
    p = dict(norm_mix=norm_mix, norm_ffn=norm_ffn, w_in=w_in, b_gate=b_gate, q_norm=q_norm,
             k_norm=k_norm, lambda_q1=lambda_q1, lambda_k1=lambda_k1, lambda_q2=lambda_q2,
             lambda_k2=lambda_k2, attn_subln=attn_subln, w_attn_branch=w_attn_branch,
             conv_w=conv_w, conv_b=conv_b, lru_wa=lru_wa, lru_ba=lru_ba, lru_wx=lru_wx,
             lru_bx=lru_bx, lru_lambda=lru_lambda, w_lru_branch=w_lru_branch, w_out=w_out,
             ffn_w_gate=ffn_w_gate, ffn_w_up=ffn_w_up, ffn_w_down=ffn_w_down,
             moe_router=moe_router, moe_w_gate=moe_w_gate, moe_w_up=moe_w_up,
             moe_w_down=moe_w_down)
    layers = _prepare(p)
    return (_trunk(x_prompt, meta_tokens, layers), _trunk(x_sample, meta_tokens, layers))
```

```python
import functools
import math

import jax
import jax.numpy as jnp
from jax import lax
from jax.experimental import pallas as pl
from jax.experimental.pallas import tpu as pltpu

F32 = jnp.float32
BF16 = jnp.bfloat16

D_MODEL = 1024
N_META = 16
N_HEADS = 8
HEAD_DIM = 64
V_DIM = 2 * HEAD_DIM
LRU_BLOCKS = 16
LRU_BLOCK_W = D_MODEL // LRU_BLOCKS
LRU_C = 8.0
N_EXPERTS = 8
EPS = 1e-6
IN_WIDTH = 7 * D_MODEL

LANE = 128
SUBLANE = 8
VMEM_LIMIT_BYTES = 48 * 1024 * 1024
NEG_BIG = -1e30

ATT_TILE = 512
ATT_MAX_UNROLL = 8
EXP_ZERO_LOGIT = 105.0
VT_ROWS = V_DIM + 16
ATT_TAIL = LANE
ATT_VMEM_LIMIT_BYTES = 56 * 1024 * 1024
MASK_LOGIT = -30000.0
SAFE_LOGIT_SPAN = 80.0

TOP_K = 2
MOE_TM = 512
MOE_TF = 512

LRU_CB = 128
LRU_TT = 128


def _largest_tile(n, target, mult=LANE):
    assert n % mult == 0, (n, mult)
    best = mult
    t = mult
    while t <= min(n, target):
        if n % t == 0:
            best = t
        t += mult
    return best


def _cparams(*sem):
    return pltpu.CompilerParams(dimension_semantics=sem, vmem_limit_bytes=VMEM_LIMIT_BYTES)


def _rmsnorm_rows(x, g):
    ms = jnp.mean(x * x, axis=-1, keepdims=True)
    return x * lax.rsqrt(ms + EPS) * g


def _in_proj_kernel(h_ref, g_ref, w_ref, z_ref, xn_ref):
    @pl.when(pl.program_id(1) == 0)
    def _():
        xn_ref[...] = _rmsnorm_rows(h_ref[...], g_ref[...]).astype(BF16)

    z_ref[...] = jnp.dot(xn_ref[...], w_ref[...], preferred_element_type=F32)


def _in_proj(h2, g, w):
    m = h2.shape[0]
    tm = _largest_tile(m, 1024)
    tn = _largest_tile(IN_WIDTH, 1792, 2 * LANE)
    return pl.pallas_call(
        _in_proj_kernel,
        grid=(m // tm, IN_WIDTH // tn),
        in_specs=[
            pl.BlockSpec((tm, D_MODEL), lambda i, j: (i, 0)),
            pl.BlockSpec((1, D_MODEL), lambda i, j: (0, 0)),
            pl.BlockSpec((D_MODEL, tn), lambda i, j: (0, j)),
        ],
        out_specs=pl.BlockSpec((tm, tn), lambda i, j: (i, j)),
        out_shape=jax.ShapeDtypeStruct((m, IN_WIDTH), F32),
        scratch_shapes=[pltpu.VMEM((tm, D_MODEL), BF16)],
        compiler_params=_cparams("parallel", "arbitrary"),
        name="in_proj",
    )(h2, g, w)


def _attn_kernel(sc_ref, q_ref, k_ref, v_ref, qg_ref, kg_ref, lam_ref, sg_ref, o_ref,
                 kb_ref, vb_ref, m_ref, l_ref, acc_ref, *, seq_len, tq, tk):
    lp = k_ref.shape[0]
    nk = lp // tk
    head = pl.program_id(1)
    qi = pl.program_id(2)
    lane = lax.broadcasted_iota(jnp.int32, (1, LANE), 1)
    lo = lane < HEAD_DIM

    def half_rmsnorm(x, g):
        sq = x * x
        s_lo = jnp.sum(jnp.where(lo, sq, 0.0), axis=-1, keepdims=True)
        s_hi = jnp.sum(jnp.where(lo, 0.0, sq), axis=-1, keepdims=True)
        ms = jnp.where(lo, s_lo, s_hi) * (1.0 / HEAD_DIM)
        return x * lax.rsqrt(ms + EPS) * g

    @pl.when(qi == 0)
    def _():
        def fill(c, carry):
            r = pl.ds(pl.multiple_of(c * tk, tk), tk)
            kb_ref[r, :] = half_rmsnorm(k_ref[r, :], kg_ref[...]).astype(BF16)
            vb_ref[r, :] = v_ref[r, :].astype(BF16)
            return carry
        lax.fori_loop(0, nk, fill, 0)

    qn = half_rmsnorm(q_ref[...], qg_ref[...]) * (1.0 / math.sqrt(HEAD_DIM))
    q_comp = (jnp.where(lo, qn, 0.0).astype(BF16), jnp.where(lo, 0.0, qn).astype(BF16))

    m_ref[...] = jnp.full(m_ref.shape, NEG_BIG, F32)
    l_ref[...] = jnp.zeros(l_ref.shape, F32)
    acc_ref[...] = jnp.zeros(acc_ref.shape, F32)

    slope = sc_ref[head]
    rel = (lax.broadcasted_iota(jnp.int32, (tq, tk), 0)
           - lax.broadcasted_iota(jnp.int32, (tq, tk), 1)).astype(F32)
    col = lax.broadcasted_iota(jnp.int32, (1, tk), 1)

    def chunk(kc, carry):
        k0 = pl.multiple_of(kc * tk, tk)
        kblk = kb_ref[pl.ds(k0, tk), :]
        vblk = vb_ref[pl.ds(k0, tk), :]
        off = (qi * tq - k0).astype(F32)
        bias = -slope * jnp.abs(rel + off) + jnp.where(col + k0 < seq_len, 0.0, NEG_BIG)
        for c in range(2):
            s = lax.dot_general(q_comp[c], kblk, (((1,), (1,)), ((), ())),
                                preferred_element_type=F32) + bias
            m_prev = m_ref[c]
            m_new = jnp.maximum(m_prev, jnp.max(s, axis=-1, keepdims=True))
            alpha = jnp.exp(m_prev - m_new)
            p = jnp.exp(s - m_new)
            l_ref[c] = alpha * l_ref[c] + jnp.sum(p, axis=-1, keepdims=True)
            acc_ref[c] = alpha * acc_ref[c] + jnp.dot(p.astype(BF16), vblk,
                                                      preferred_element_type=F32)
            m_ref[c] = m_new
        return carry

    lax.fori_loop(0, nk, chunk, 0)

    lam_init = sc_ref[N_HEADS]
    lp_ = lam_ref[...]
    lam = (jnp.exp(jnp.sum(lp_[0:1] * lp_[1:2], axis=-1, keepdims=True))
           - jnp.exp(jnp.sum(lp_[2:3] * lp_[3:4], axis=-1, keepdims=True)) + lam_init)
    o = acc_ref[0] / l_ref[0] - lam * (acc_ref[1] / l_ref[1])
    o = _rmsnorm_rows(o, sg_ref[...]) * (1.0 - lam_init)
    o_ref[...] = o.astype(BF16)


def _attn_bounded_kernel(sc_ref, q_ref, k_ref, v_ref, qg_ref, kg_ref, lam_ref, sg_ref, prev_ref,
                         o_ref, kq_ref, vt_ref, vtail_ref, lhs_ref, bias_ref, acc_ref,
                         *, seq_len, t, head0, reach):
    lp = k_ref.shape[0]
    s_main = lp - ATT_TAIL
    n_main = s_main // t
    del prev_ref
    slope = sc_ref[head0 + pl.program_id(1)]
    lam_init = sc_ref[N_HEADS]
    lane = lax.broadcasted_iota(jnp.int32, (1, LANE), 1)
    lo = lane < HEAD_DIM
    nt = (((1,), (1,)), ((), ()))

    def half_rmsnorm(x, g):
        sq = x * x
        s_lo = jnp.sum(jnp.where(lo, sq, 0.0), axis=-1, keepdims=True)
        s_hi = jnp.sum(jnp.where(lo, 0.0, sq), axis=-1, keepdims=True)
        ms = jnp.where(lo, s_lo, s_hi) * (1.0 / HEAD_DIM)
        return x * lax.rsqrt(ms + EPS) * g

    def split2(x):
        hi = x.astype(BF16).astype(F32)
        return hi, x - hi

    def pos_col(r0, n):
        return lax.broadcasted_iota(jnp.int32, (n, 1), 0) + r0

    lp_ = lam_ref[...]
    lam = (jnp.exp(jnp.sum(lp_[0:1] * lp_[1:2], axis=-1, keepdims=True))
           - jnp.exp(jnp.sum(lp_[2:3] * lp_[3:4], axis=-1, keepdims=True)) + lam_init)

    ones_row = jnp.where(lax.broadcasted_iota(jnp.int32, (VT_ROWS - V_DIM, 1), 0) == 0, 1.0, 0.0)

    def fill(r0, n, kmax, vt_dst):
        rows = pos_col(r0, n)
        valid = rows < seq_len
        kn = jnp.where(valid, half_rmsnorm(k_ref[pl.ds(r0, n), :], kg_ref[...]), 0.0)
        sq = kn * kn
        n_lo = jnp.max(jnp.sum(jnp.where(lo, sq, 0.0), axis=-1, keepdims=True), axis=0, keepdims=True)
        n_hi = jnp.max(jnp.sum(jnp.where(lo, 0.0, sq), axis=-1, keepdims=True), axis=0, keepdims=True)
        sj_hi, sj_lo = split2(slope * rows.astype(F32))
        aug = jnp.where(lane == 0, -1.0,
              jnp.where((lane == 1) | (lane == 2), 1.0,
              jnp.where(lane == 3, sj_hi,
              jnp.where(lane == 4, sj_lo,
              jnp.where(lane == 5, jnp.where(valid, 0.0, MASK_LOGIT), 0.0)))))
        kq_ref[pl.ds(r0, n), 0:LANE] = kn.astype(BF16)
        kq_ref[pl.ds(r0, n), LANE:2 * LANE] = aug.astype(BF16)
        vt = jnp.transpose(jnp.where(valid, v_ref[pl.ds(r0, n), :], 0.0))
        vt_dst[...] = jnp.concatenate(
            [vt, jnp.broadcast_to(ones_row, (VT_ROWS - V_DIM, n))], axis=0).astype(BF16)
        return jnp.maximum(kmax[0], n_lo), jnp.maximum(kmax[1], n_hi)

    zero11 = jnp.zeros((1, 1), F32)
    kmax = lax.fori_loop(
        0, n_main, lambda c, km: fill(pl.multiple_of(c * t, t), t, km, vt_ref.at[c]),
        (zero11, zero11))
    kmax = fill(s_main, ATT_TAIL, kmax, vtail_ref)
    knorm = (jnp.sqrt(kmax[0]), jnp.sqrt(kmax[1]))

    rel = (lax.broadcasted_iota(jnp.int32, (t, t), 0)
           - lax.broadcasted_iota(jnp.int32, (t, t), 1)).astype(F32)
    bias_ref[0] = jnp.zeros((t, t), F32)
    bias_ref[1] = -slope * jnp.abs(rel)

    def build_lhs(r0, n):
        rows = pos_col(r0, n)
        qn = half_rmsnorm(q_ref[pl.ds(r0, n), :], qg_ref[...]) * (1.0 / math.sqrt(HEAD_DIM))
        si_hi, si_lo = split2(slope * rows.astype(F32))
        side = jnp.where(lane == 1, si_hi,
               jnp.where(lane == 2, si_lo,
               jnp.where((lane == 3) | (lane == 4), -1.0, 0.0)))
        for c in range(2):
            qc = jnp.where(lo, qn, 0.0) if c == 0 else jnp.where(lo, 0.0, qn)
            bound = jnp.sqrt(jnp.sum(qc * qc, axis=-1, keepdims=True)) * knorm[c]
            base = jnp.where(lane == 0, bound, jnp.where(lane == 5, 1.0, 0.0))
            qcb = qc.astype(BF16)
            for x, aug in enumerate((base - side, base, base + side)):
                lhs_ref[x, c * n:(c + 1) * n, 0:LANE] = qcb
                lhs_ref[x, c * n:(c + 1) * n, LANE:2 * LANE] = aug.astype(BF16)

    def process(x, n, k0, nk, vt, bias):
        s = lax.dot_general(kq_ref[pl.ds(k0, nk), :], lhs_ref[x, 0:2 * n, :], nt,
                            preferred_element_type=F32)
        if bias is not None:
            s = s + jnp.concatenate([bias, bias], axis=1)
        p = jnp.exp(s).astype(BF16)
        acc_ref[:, 0:2 * n] += jnp.dot(vt, p, preferred_element_type=F32)

    def finalize(r0, n):
        a0 = acc_ref[:, 0:n]
        a1 = acc_ref[:, n:2 * n]
        o = (a0[0:V_DIM] * (1.0 / a0[V_DIM:V_DIM + 1])
             - (lam * (1.0 / a1[V_DIM:V_DIM + 1])) * a1[0:V_DIM])
        ms = jnp.mean(o * o, axis=0, keepdims=True)
        o = o * lax.rsqrt(ms + EPS) * sg_ref[...] * (1.0 - lam_init)
        pos = lax.broadcasted_iota(jnp.int32, (1, n), 1) + r0
        o = jnp.where(pos < seq_len, o, 0.0)
        o_ref[pl.ds(r0, n), :] = jnp.transpose(o).astype(BF16)

    window = min(2 * reach + 1, n_main)

    def full_chunks(qi, n, kc0, count):
        def body(w, carry):
            kc = kc0 + w
            x = jnp.where(kc < qi, 0, jnp.where(kc == qi, 1, 2))
            d = jnp.where(kc == qi, 1, 0)
            process(x, n, pl.multiple_of(kc * t, t), t, vt_ref[kc], bias_ref[d, :, 0:n])
            return carry
        lax.fori_loop(0, count, body, 0, unroll=min(count, ATT_MAX_UNROLL))

    def q_main(qi, carry):
        r0 = pl.multiple_of(qi * t, t)
        build_lhs(r0, t)
        acc_ref[...] = jnp.zeros(acc_ref.shape, F32)
        full_chunks(qi, t, jnp.clip(qi - reach, 0, n_main - window), window)

        @pl.when(n_main - qi <= reach)
        def _():
            process(2, t, s_main, ATT_TAIL, vtail_ref[...], None)
        finalize(r0, t)
        return carry
    lax.fori_loop(0, n_main, q_main, 0)

    build_lhs(s_main, ATT_TAIL)
    acc_ref[...] = jnp.zeros(acc_ref.shape, F32)
    tail_count = min(reach, n_main)
    full_chunks(n_main, ATT_TAIL, n_main - tail_count, tail_count)
    process(1, ATT_TAIL, s_main, ATT_TAIL, vtail_ref[...], bias_ref[1, 0:ATT_TAIL, 0:ATT_TAIL])
    finalize(s_main, ATT_TAIL)


def _chunk_reach(head, t):
    slope = 2.0 ** (-8.0 * (head + 1) / N_HEADS)
    return max(1, math.ceil((EXP_ZERO_LOGIT / slope - 1.0) / t))


def _attention_bounded(z3, scal, qg, kg, lamp, sg, seq_len):
    b, lp, _ = z3.shape
    t = _largest_tile(lp - ATT_TAIL, ATT_TILE)
    n_main = (lp - ATT_TAIL) // t
    groups = []
    for head in range(N_HEADS):
        reach = min(_chunk_reach(head, t), n_main)
        if groups and min(2 * groups[-1][2] + 1, n_main) == min(2 * reach + 1, n_main):
            groups[-1][1] += 1
            groups[-1][2] = max(groups[-1][2], reach)
        else:
            groups.append([head, 1, reach])
    small = lambda shape: pl.BlockSpec(shape, lambda bi, hi: (0, 0))
    att = jnp.zeros((b, lp, N_HEADS * V_DIM), BF16)
    for head0, count, reach in groups:
        kern = functools.partial(_attn_bounded_kernel, seq_len=seq_len, t=t, head0=head0,
                                 reach=reach)
        col = lambda off, head0=head0: pl.BlockSpec(
            (None, lp, V_DIM), lambda bi, hi: (bi, 0, off + head0 + hi))
        att = pl.pallas_call(
            kern,
            grid=(b, count),
            in_specs=[
                pl.BlockSpec(memory_space=pltpu.SMEM),
                col(0), col(N_HEADS), col(2 * N_HEADS),
                small((1, V_DIM)), small((1, V_DIM)), small((4, HEAD_DIM)), small((V_DIM, 1)),
                pl.BlockSpec(memory_space=pl.ANY),
            ],
            out_specs=col(0),
            out_shape=jax.ShapeDtypeStruct(att.shape, att.dtype),
            input_output_aliases={8: 0},
            scratch_shapes=[
                pltpu.VMEM((lp, 2 * LANE), BF16),
                pltpu.VMEM((n_main, VT_ROWS, t), BF16), pltpu.VMEM((VT_ROWS, ATT_TAIL), BF16),
                pltpu.VMEM((3, 2 * t, 2 * LANE), BF16),
                pltpu.VMEM((2, t, t), F32),
                pltpu.VMEM((VT_ROWS, 2 * t), F32),
            ],
            compiler_params=pltpu.CompilerParams(dimension_semantics=("parallel", "parallel"),
                                                 vmem_limit_bytes=ATT_VMEM_LIMIT_BYTES),
            name="diff_attention_bounded",
        )(scal, z3, z3, z3, qg, kg, lamp, sg.reshape(V_DIM, 1), att)
    return att


def _attention(z3, scal, qg, kg, lamp, sg, seq_len):
    span = 2.0 * HEAD_DIM / math.sqrt(HEAD_DIM) * jnp.max(jnp.abs(qg)) * jnp.max(jnp.abs(kg))
    args = (z3, scal, qg, kg, lamp, sg)
    return lax.cond(span <= SAFE_LOGIT_SPAN,
                    lambda *a: _attention_bounded(*a, seq_len),
                    lambda *a: _attention_online(*a, seq_len), *args)


def _attention_online(z3, scal, qg, kg, lamp, sg, seq_len):
    b, lp, _ = z3.shape
    tq = _largest_tile(lp, 768)
    tk = tq
    kern = functools.partial(_attn_kernel, seq_len=seq_len, tq=tq, tk=tk)
    small = lambda shape: pl.BlockSpec(shape, lambda bi, hi, qi: (0, 0))
    return pl.pallas_call(
        kern,
        grid=(b, N_HEADS, lp // tq),
        in_specs=[
            pl.BlockSpec(memory_space=pltpu.SMEM),
            pl.BlockSpec((None, tq, V_DIM), lambda bi, hi, qi: (bi, qi, hi)),
            pl.BlockSpec((None, lp, V_DIM), lambda bi, hi, qi: (bi, 0, N_HEADS + hi)),
            pl.BlockSpec((None, lp, V_DIM), lambda bi, hi, qi: (bi, 0, 2 * N_HEADS + hi)),
            small((1, V_DIM)), small((1, V_DIM)), small((4, HEAD_DIM)), small((1, V_DIM)),
        ],
        out_specs=pl.BlockSpec((None, tq, V_DIM), lambda bi, hi, qi: (bi, qi, hi)),
        out_shape=jax.ShapeDtypeStruct((b, lp, N_HEADS * V_DIM), BF16),
        scratch_shapes=[
            pltpu.VMEM((lp, V_DIM), BF16), pltpu.VMEM((lp, V_DIM), BF16),
            pltpu.VMEM((2, tq, 1), F32), pltpu.VMEM((2, tq, 1), F32),
            pltpu.VMEM((2, tq, V_DIM), F32),
        ],
        compiler_params=_cparams("parallel", "parallel", "arbitrary"),
        name="diff_attention",
    )(scal, z3, z3, z3, qg, kg, lamp, sg)


def _lru_kernel(xl_ref, gl_ref, cw_ref, cbias_ref, w_ref, ba_ref, bx_ref, lam_ref, o_ref,
                xp_ref, hf_ref, hb_ref, *, seq_len, tt):
    lp, cb = xl_ref.shape
    n = lp // tt
    pad = 8
    rows_t = lax.broadcasted_iota(jnp.int32, (tt, 1), 0)

    xp_ref[0:pad, :] = jnp.zeros((pad, cb), F32)
    xp_ref[lp + pad:lp + 2 * pad, :] = jnp.zeros((pad, cb), F32)

    def copy_in(c, carry):
        r0 = pl.multiple_of(c * tt, tt)
        x = xl_ref[pl.ds(r0, tt), :]
        xp_ref[pl.ds(r0 + pad, tt), :] = jnp.where(rows_t + r0 < seq_len, x, 0.0)
        return carry
    lax.fori_loop(0, n, copy_in, 0)

    cw = cw_ref[...]

    def gates(c, d):
        r0 = pl.multiple_of(c * tt, tt)
        win = xp_ref[pl.ds(r0, tt + 2 * pad), :]
        wn = tt + 2 * pad
        xc = (pltpu.roll(win, 2, 0)[pad:pad + tt] * cw[0:1]
              + pltpu.roll(win, 1, 0)[pad:pad + tt] * cw[1:2]
              + win[pad:pad + tt] * cw[2:3]
              + pltpu.roll(win, wn - 1, 0)[pad:pad + tt] * cw[3:4]
              + cbias_ref[...])
        pre = jnp.dot(xc.astype(BF16), w_ref[:, d * 2 * cb:(d + 1) * 2 * cb],
                      preferred_element_type=F32)
        r = jax.nn.sigmoid(pre[:, :cb] + ba_ref[d:d + 1, :])
        i = jax.nn.sigmoid(pre[:, cb:] + bx_ref[d:d + 1, :])
        nl = -lam_ref[d:d + 1, :]
        softplus = jnp.maximum(nl, 0.0) + jnp.log1p(jnp.exp(-jnp.abs(nl)))
        log_a = (-LRU_C * r) * softplus
        a = jnp.exp(log_a)
        th = jnp.tanh(log_a)
        mult = jnp.sqrt(-2.0 * th / (1.0 - th))
        u = mult * (i * xc)
        u = jnp.where(rows_t + r0 < seq_len, u, 0.0)
        return a, u

    ng = tt // SUBLANE
    sub = lax.broadcasted_iota(jnp.int32, (1, SUBLANE, 1), 1)

    def scan_chunk(a, u, carry, reverse):
        a3 = a.reshape(ng, SUBLANE, cb)
        u3 = u.reshape(ng, SUBLANE, cb)
        d = 1
        while d < SUBLANE:
            if reverse:
                shift, keep = SUBLANE - d, sub < SUBLANE - d
            else:
                shift, keep = d, sub >= d
            u3 = u3 + a3 * jnp.where(keep, pltpu.roll(u3, shift, 1), 0.0)
            a3 = a3 * jnp.where(keep, pltpu.roll(a3, shift, 1), 1.0)
            d *= 2
        hs = [None] * ng
        for g in (range(ng - 1, -1, -1) if reverse else range(ng)):
            hg = u3[g] + a3[g] * carry
            hs[g] = hg
            carry = hg[0:1, :] if reverse else hg[SUBLANE - 1:SUBLANE, :]
        return jnp.concatenate(hs, axis=0), carry

    def both(j, carry):
        cf = j
        h, state_f = scan_chunk(*gates(cf, 0), carry[0], False)
        hf_ref[pl.ds(pl.multiple_of(cf * tt, tt), tt), :] = h
        cr = n - 1 - j
        h, state_b = scan_chunk(*gates(cr, 1), carry[1], True)
        hb_ref[pl.ds(pl.multiple_of(cr * tt, tt), tt), :] = h
        return state_f, state_b
    zero_state = jnp.zeros((1, cb), F32)
    lax.fori_loop(0, n, both, (zero_state, zero_state))

    def combine(c, carry):
        r = pl.ds(pl.multiple_of(c * tt, tt), tt)
        o_ref[r, :] = ((hf_ref[r, :] + hb_ref[r, :]) * jax.nn.gelu(gl_ref[r, :])).astype(BF16)
        return carry
    lax.fori_loop(0, n, combine, 0)


def _lru(z3, cw, cbias, wcat, ba, bx, lam, seq_len):
    b, lp, _ = z3.shape
    cb = LRU_CB
    nc = D_MODEL // cb
    kern = functools.partial(_lru_kernel, seq_len=seq_len, tt=LRU_TT)
    chan = lambda rows: pl.BlockSpec((rows, cb), lambda bi, ci: (0, ci))
    return pl.pallas_call(
        kern,
        grid=(b, nc),
        in_specs=[
            pl.BlockSpec((None, lp, cb), lambda bi, ci: (bi, 0, 3 * nc + ci)),
            pl.BlockSpec((None, lp, cb), lambda bi, ci: (bi, 0, 4 * nc + ci)),
            chan(4), chan(1),
            pl.BlockSpec((None, cb, 4 * cb), lambda bi, ci: (ci, 0, 0)),
            chan(2), chan(2), chan(2),
        ],
        out_specs=pl.BlockSpec((None, lp, cb), lambda bi, ci: (bi, 0, ci)),
        out_shape=jax.ShapeDtypeStruct((b, lp, D_MODEL), BF16),
        scratch_shapes=[pltpu.VMEM((lp + 16, cb), F32), pltpu.VMEM((lp, cb), F32),
                        pltpu.VMEM((lp, cb), F32)],
        compiler_params=_cparams("parallel", "parallel"),
        name="rglru",
    )(z3, z3, cw, cbias, wcat, ba, bx, lam)


def _merge_kernel(att_ref, yl_ref, ga_ref, gr_ref, h_ref, wa_ref, wl_ref, wo_ref,
                  bga_ref, bgr_ref, o_ref):
    ya = jnp.dot(att_ref[...], wa_ref[...], preferred_element_type=F32)
    yr = jnp.dot(yl_ref[...], wl_ref[...], preferred_element_type=F32)
    merged = (jax.nn.sigmoid(ga_ref[...] + bga_ref[...]) * ya
              + jax.nn.sigmoid(gr_ref[...] + bgr_ref[...]) * yr)
    o_ref[...] = h_ref[...] + jnp.dot(merged.astype(BF16), wo_ref[...],
                                      preferred_element_type=F32)


def _merge(att2, yl2, z2, h2, wa, wl, wo, bga, bgr):
    m = h2.shape[0]
    tm = _largest_tile(m, 256)
    row = lambda col: pl.BlockSpec((tm, D_MODEL), lambda i: (i, col))
    full = lambda shape: pl.BlockSpec(shape, lambda i: (0, 0))
    return pl.pallas_call(
        _merge_kernel,
        grid=(m // tm,),
        in_specs=[row(0), row(0), row(5), row(6), row(0),
                  full((D_MODEL, D_MODEL)), full((D_MODEL, D_MODEL)), full((D_MODEL, D_MODEL)),
                  full((1, D_MODEL)), full((1, D_MODEL))],
        out_specs=row(0),
        out_shape=jax.ShapeDtypeStruct((m, D_MODEL), F32),
        compiler_params=_cparams("parallel"),
        name="merge_out_proj",
    )(att2, yl2, z2, z2, h2, wa, wl, wo, bga, bgr)


def _ffn_kernel(*refs, chunks_per_expert):
    if chunks_per_expert:
        h_ref, g_ref, wr_ref, wg_ref, wu_ref, wd_ref, o_ref, hn_ref, acc_ref, gate_ref = refs
    else:
        h_ref, g_ref, wg_ref, wu_ref, wd_ref, o_ref, hn_ref, acc_ref = refs
    j = pl.program_id(1)

    @pl.when(j == 0)
    def _():
        h = h_ref[...]
        hn = _rmsnorm_rows(h, g_ref[...])
        hn_ref[...] = hn.astype(BF16)
        acc_ref[...] = h
        if chunks_per_expert:
            logits = jnp.dot(hn, wr_ref[...], preferred_element_type=F32,
                             precision=lax.Precision.HIGHEST)
            lane = lax.broadcasted_iota(jnp.int32, logits.shape, 1).astype(F32)
            lg = jnp.where(lane < N_EXPERTS, logits, NEG_BIG)
            m1 = jnp.max(lg, axis=-1, keepdims=True)
            i1 = jnp.min(jnp.where(lg == m1, lane, float(LANE)), axis=-1, keepdims=True)
            lg2 = jnp.where(lane == i1, NEG_BIG, lg)
            m2 = jnp.max(lg2, axis=-1, keepdims=True)
            i2 = jnp.min(jnp.where(lg2 == m2, lane, float(LANE)), axis=-1, keepdims=True)
            e2 = jnp.exp(m2 - m1)
            den = 1.0 + e2
            gate_ref[...] = jnp.where(lane == i1, 1.0 / den,
                                      jnp.where(lane == i2, e2 / den, 0.0))

    hn = hn_ref[...]
    gg = jnp.dot(hn, wg_ref[...], preferred_element_type=F32)
    uu = jnp.dot(hn, wu_ref[...], preferred_element_type=F32)
    act = (jax.nn.silu(gg) * uu).astype(BF16)
    y = jnp.dot(act, wd_ref[...], preferred_element_type=F32)
    if chunks_per_expert:
        e = (j // chunks_per_expert).astype(F32)
        gates = gate_ref[...]
        lane = lax.broadcasted_iota(jnp.int32, gates.shape, 1).astype(F32)
        y = y * jnp.sum(jnp.where(lane == e, gates, 0.0), axis=-1, keepdims=True)
    acc_ref[...] += y

    @pl.when(j == pl.num_programs(1) - 1)
    def _():
        o_ref[...] = acc_ref[...]


def _ffn(h2, g, wg, wu, wd, router=None, d_ff_expert=None):
    m = h2.shape[0]
    f = wg.shape[1]
    tm = _largest_tile(m, 1024)
    tf = 512
    in_specs = [pl.BlockSpec((tm, D_MODEL), lambda i, j: (i, 0)),
                pl.BlockSpec((1, D_MODEL), lambda i, j: (0, 0))]
    args = [h2, g]
    scratch = [pltpu.VMEM((tm, D_MODEL), BF16), pltpu.VMEM((tm, D_MODEL), F32)]
    cpe = 0
    if router is not None:
        assert d_ff_expert % tf == 0
        cpe = d_ff_expert // tf
        in_specs.append(pl.BlockSpec((D_MODEL, LANE), lambda i, j: (0, 0)))
        args.append(router)
        scratch.append(pltpu.VMEM((tm, LANE), F32))
    in_specs += [pl.BlockSpec((D_MODEL, tf), lambda i, j: (0, j)),
                 pl.BlockSpec((D_MODEL, tf), lambda i, j: (0, j)),
                 pl.BlockSpec((tf, D_MODEL), lambda i, j: (j, 0))]
    args += [wg, wu, wd]
    return pl.pallas_call(
        functools.partial(_ffn_kernel, chunks_per_expert=cpe),
        grid=(m // tm, f // tf),
        in_specs=in_specs,
        out_specs=pl.BlockSpec((tm, D_MODEL), lambda i, j: (i, 0)),
        out_shape=jax.ShapeDtypeStruct((m, D_MODEL), F32),
        scratch_shapes=scratch,
        compiler_params=_cparams("parallel", "arbitrary"),
        name="moe_ffn" if cpe else "dense_ffn",
    )(*args)


def _moe_router_kernel(h_ref, g_ref, wr_ref, r_ref, cnt_ref, tri_ref, base_ref):
    tm = h_ref.shape[0]

    @pl.when(pl.program_id(0) == 0)
    def _():
        row = lax.broadcasted_iota(jnp.int32, (tm, tm), 0)
        col = lax.broadcasted_iota(jnp.int32, (tm, tm), 1)
        tri_ref[...] = jnp.where(col < row, 1.0, 0.0).astype(BF16)
        base_ref[...] = jnp.zeros(base_ref.shape, F32)

    hn = _rmsnorm_rows(h_ref[...], g_ref[...])
    logits = jnp.dot(hn, wr_ref[...], preferred_element_type=F32, precision=lax.Precision.HIGHEST)
    lane = lax.broadcasted_iota(jnp.int32, logits.shape, 1).astype(F32)
    lg = jnp.where(lane < N_EXPERTS, logits, NEG_BIG)
    m1 = jnp.max(lg, axis=-1, keepdims=True)
    i1 = jnp.min(jnp.where(lg == m1, lane, float(LANE)), axis=-1, keepdims=True)
    lg2 = jnp.where(lane == i1, NEG_BIG, lg)
    m2 = jnp.max(lg2, axis=-1, keepdims=True)
    i2 = jnp.min(jnp.where(lg2 == m2, lane, float(LANE)), axis=-1, keepdims=True)
    e2 = jnp.exp(m2 - m1)
    den = 1.0 + e2
    oh1 = lane == i1
    oh2 = lane == i2
    assigned = jnp.where(oh1 | oh2, 1.0, 0.0)
    before = jnp.dot(tri_ref[...], assigned.astype(BF16), preferred_element_type=F32) + base_ref[...]
    rank1 = jnp.sum(jnp.where(oh1, before, 0.0), axis=-1, keepdims=True)
    rank2 = jnp.sum(jnp.where(oh2, before, 0.0), axis=-1, keepdims=True)
    r_ref[...] = jnp.where(lane == 0, i1, jnp.where(lane == 1, i2,
                 jnp.where(lane == 2, 1.0 / den, jnp.where(lane == 3, e2 / den,
                 jnp.where(lane == 4, rank1, jnp.where(lane == 5, rank2, 0.0))))))
    base_ref[...] += jnp.sum(assigned, axis=0, keepdims=True)
    cnt_ref[...] = base_ref[...]


def _moe_router(h2, g, router):
    m = h2.shape[0]
    tm = _largest_tile(m, 1024)
    return pl.pallas_call(
        _moe_router_kernel,
        grid=(m // tm,),
        in_specs=[pl.BlockSpec((tm, D_MODEL), lambda i: (i, 0)),
                  pl.BlockSpec((1, D_MODEL), lambda i: (0, 0)),
                  pl.BlockSpec((D_MODEL, LANE), lambda i: (0, 0))],
        out_specs=[pl.BlockSpec((tm, LANE), lambda i: (i, 0)),
                   pl.BlockSpec((1, LANE), lambda i: (0, 0))],
        out_shape=[jax.ShapeDtypeStruct((m, LANE), F32), jax.ShapeDtypeStruct((1, LANE), F32)],
        scratch_shapes=[pltpu.VMEM((tm, tm), BF16), pltpu.VMEM((1, LANE), F32)],
        compiler_params=_cparams("arbitrary"),
        name="moe_router",
    )(h2, g, router)


def _row_copy(src_ref, src_row, dst_ref, dst_row, sem):
    return pltpu.make_async_copy(src_ref.at[pl.ds(src_row, 1), :], dst_ref.at[pl.ds(dst_row, 1), :], sem)


def _moe_scatter_kernel(pos_ref, h_ref, g_ref, xs_in_ref, xs_ref, hn_ref, sem):
    del xs_in_ref
    tm = h_ref.shape[0]
    hn_ref[...] = _rmsnorm_rows(h_ref[...], g_ref[...])

    def body(r, carry):
        for s in range(TOP_K):
            _row_copy(hn_ref, r, xs_ref, pos_ref[r, s], sem).start()
        return carry
    lax.fori_loop(0, tm, body, 0)
    for s in range(TOP_K):
        pltpu.make_async_copy(hn_ref, xs_ref.at[pl.ds(0, tm), :], sem).wait()


def _moe_scatter(pos, h2, g, n_rows):
    m = h2.shape[0]
    tm = _largest_tile(m, MOE_TM)
    xs0 = jnp.zeros((n_rows, D_MODEL), F32)
    return pl.pallas_call(
        _moe_scatter_kernel,
        grid=(m // tm,),
        in_specs=[pl.BlockSpec((tm, TOP_K), lambda i: (i, 0), memory_space=pltpu.SMEM),
                  pl.BlockSpec((tm, D_MODEL), lambda i: (i, 0)),
                  pl.BlockSpec((1, D_MODEL), lambda i: (0, 0)),
                  pl.BlockSpec(memory_space=pl.ANY)],
        out_specs=pl.BlockSpec(memory_space=pl.ANY),
        out_shape=jax.ShapeDtypeStruct(xs0.shape, xs0.dtype),
        input_output_aliases={3: 0},
        scratch_shapes=[pltpu.VMEM((tm, D_MODEL), F32), pltpu.SemaphoreType.DMA(())],
        compiler_params=_cparams("arbitrary"),
        name="moe_scatter",
    )(pos, h2, g, xs0)


def _moe_expert_kernel(te_ref, x_ref, wg_ref, wu_ref, wd_ref, y_ref, xb_ref, acc_ref):
    del te_ref
    j = pl.program_id(1)

    @pl.when(j == 0)
    def _():
        xb_ref[...] = x_ref[...].astype(BF16)
        acc_ref[...] = jnp.zeros(acc_ref.shape, F32)

    xb = xb_ref[...]
    gg = jnp.dot(xb, wg_ref[...], preferred_element_type=F32)
    uu = jnp.dot(xb, wu_ref[...], preferred_element_type=F32)
    act = (jax.nn.silu(gg) * uu).astype(BF16)
    acc_ref[...] += jnp.dot(act, wd_ref[...], preferred_element_type=F32)

    @pl.when(j == pl.num_programs(1) - 1)
    def _():
        y_ref[...] = acc_ref[...]


def _moe_experts(tile_expert, xs, wg, wu, wd):
    n_rows = xs.shape[0]
    fe = wg.shape[2]
    tm, tf = MOE_TM, MOE_TF
    grid_spec = pltpu.PrefetchScalarGridSpec(
        num_scalar_prefetch=1,
        grid=(n_rows // tm, fe // tf),
        in_specs=[pl.BlockSpec((tm, D_MODEL), lambda i, j, te: (i, 0)),
                  pl.BlockSpec((None, D_MODEL, tf), lambda i, j, te: (te[i], 0, j)),
                  pl.BlockSpec((None, D_MODEL, tf), lambda i, j, te: (te[i], 0, j)),
                  pl.BlockSpec((None, tf, D_MODEL), lambda i, j, te: (te[i], j, 0))],
        out_specs=pl.BlockSpec((tm, D_MODEL), lambda i, j, te: (i, 0)),
        scratch_shapes=[pltpu.VMEM((tm, D_MODEL), BF16), pltpu.VMEM((tm, D_MODEL), F32)])
    return pl.pallas_call(
        _moe_expert_kernel,
        grid_spec=grid_spec,
        out_shape=jax.ShapeDtypeStruct((n_rows, D_MODEL), F32),
        compiler_params=_cparams("parallel", "arbitrary"),
        name="moe_experts",
    )(tile_expert, xs, wg, wu, wd)


def _moe_combine_kernel(pos_ref, r_ref, h_ref, ys_ref, o_ref, y_ref, sem):
    tm = h_ref.shape[0]

    def body(r, carry):
        for s in range(TOP_K):
            _row_copy(ys_ref, pos_ref[r, s], y_ref.at[s], r, sem).start()
        return carry
    lax.fori_loop(0, tm, body, 0)
    for s in range(TOP_K):
        pltpu.make_async_copy(ys_ref.at[pl.ds(0, tm), :], y_ref.at[s], sem).wait()
    r = r_ref[...]
    o_ref[...] = h_ref[...] + (r[:, 2:3] * y_ref[0] + r[:, 3:4] * y_ref[1])


def _moe_combine(pos, route, h2, ys):
    m = h2.shape[0]
    tm = _largest_tile(m, MOE_TM)
    return pl.pallas_call(
        _moe_combine_kernel,
        grid=(m // tm,),
        in_specs=[pl.BlockSpec((tm, TOP_K), lambda i: (i, 0), memory_space=pltpu.SMEM),
                  pl.BlockSpec((tm, LANE), lambda i: (i, 0)),
                  pl.BlockSpec((tm, D_MODEL), lambda i: (i, 0)),
                  pl.BlockSpec(memory_space=pl.ANY)],
        out_specs=pl.BlockSpec((tm, D_MODEL), lambda i: (i, 0)),
        out_shape=jax.ShapeDtypeStruct((m, D_MODEL), F32),
        scratch_shapes=[pltpu.VMEM((TOP_K, tm, D_MODEL), F32), pltpu.SemaphoreType.DMA(())],
        compiler_params=_cparams("arbitrary"),
        name="moe_combine",
    )(pos, route, h2, ys)


def _moe(h2, g, router, wg, wu, wd):
    m = h2.shape[0]
    route, counts = _moe_router(h2, g, router)
    cnt = counts[0, :N_EXPERTS].astype(jnp.int32)
    padded = (cnt + MOE_TM - 1) // MOE_TM * MOE_TM
    ends = jnp.cumsum(padded)
    starts = ends - padded
    expert = route[:, 0:TOP_K].astype(jnp.int32)
    pos = starts[expert] + route[:, 4:4 + TOP_K].astype(jnp.int32)
    n_rows = TOP_K * m + N_EXPERTS * MOE_TM
    tile_start = jnp.arange(n_rows // MOE_TM, dtype=jnp.int32) * MOE_TM
    tile_expert = jnp.minimum(jnp.searchsorted(ends, tile_start, side='right'),
                              N_EXPERTS - 1).astype(jnp.int32)
    xs = _moe_scatter(pos, h2, g, n_rows)
    ys = _moe_experts(tile_expert, xs, wg, wu, wd)
    return _moe_combine(pos, route, h2, ys)


def _blockdiag_groups(w, per):
    g = LRU_BLOCKS // per
    w = w.reshape(g, per, LRU_BLOCK_W, LRU_BLOCK_W)
    eye = jnp.eye(per, dtype=w.dtype)
    out = jnp.einsum('gpio,pq->gpiqo', w, eye)
    return out.reshape(g, per * LRU_BLOCK_W, per * LRU_BLOCK_W)


def _prepare(p):
    depth = p['w_in'].shape[0]
    per = LRU_CB // LRU_BLOCK_W
    layers = []
    slopes = jnp.exp2(-8.0 * jnp.arange(1, N_HEADS + 1, dtype=F32) / N_HEADS)
    for l in range(depth):
        lam_init = 0.8 - 0.6 * math.exp(-0.3 * l)
        wcat = jnp.concatenate(
            [_blockdiag_groups(p['lru_wa'][l, 0], per), _blockdiag_groups(p['lru_wx'][l, 0], per),
             _blockdiag_groups(p['lru_wa'][l, 1], per), _blockdiag_groups(p['lru_wx'][l, 1], per)],
            axis=-1).astype(BF16)
        lay = dict(
            norm_mix=p['norm_mix'][l][None], norm_ffn=p['norm_ffn'][l][None],
            w_in=p['w_in'][l].astype(BF16),
            scal=jnp.concatenate([slopes, jnp.full((N_HEADS,), lam_init, F32)]),
            qg=jnp.tile(p['q_norm'][l], 2)[None], kg=jnp.tile(p['k_norm'][l], 2)[None],
            lamp=jnp.stack([p['lambda_q1'][l], p['lambda_k1'][l],
                            p['lambda_q2'][l], p['lambda_k2'][l]]),
            sg=p['attn_subln'][l][None],
            conv_w=p['conv_w'][l], conv_b=p['conv_b'][l][None], wcat=wcat,
            ba=p['lru_ba'][l], bx=p['lru_bx'][l], lam=p['lru_lambda'][l],
            wa=p['w_attn_branch'][l].astype(BF16), wl=p['w_lru_branch'][l].astype(BF16),
            wo=p['w_out'][l].astype(BF16),
            bga=p['b_gate'][l, :D_MODEL][None], bgr=p['b_gate'][l, D_MODEL:][None],
        )
        j = l // 2
        if l % 2 == 0:
            lay.update(wg=p['ffn_w_gate'][j].astype(BF16), wu=p['ffn_w_up'][j].astype(BF16),
                       wd=p['ffn_w_down'][j].astype(BF16))
        else:
            lay.update(
                wg=p['moe_w_gate'][j].astype(BF16), wu=p['moe_w_up'][j].astype(BF16),
                wd=p['moe_w_down'][j].astype(BF16),
                router=jnp.pad(p['moe_router'][j], ((0, 0), (0, LANE - N_EXPERTS))))
        layers.append(lay)
    return layers


def _trunk(x, meta, layers):
    b, s, _ = x.shape
    seq_len = s + N_META
    lp = -(-seq_len // LANE) * LANE
    h = jnp.concatenate(
        [jnp.broadcast_to(meta[None].astype(x.dtype), (b, N_META, D_MODEL)), x,
         jnp.zeros((b, lp - seq_len, D_MODEL), x.dtype)], axis=1)
    h2 = h.reshape(b * lp, D_MODEL)
    for lay in layers:
        z2 = _in_proj(h2, lay['norm_mix'], lay['w_in'])
        z3 = z2.reshape(b, lp, IN_WIDTH)
        att = _attention(z3, lay['scal'], lay['qg'], lay['kg'], lay['lamp'], lay['sg'], seq_len)
        yl = _lru(z3, lay['conv_w'], lay['conv_b'], lay['wcat'], lay['ba'], lay['bx'],
                  lay['lam'], seq_len)
        h2 = _merge(att.reshape(b * lp, D_MODEL), yl.reshape(b * lp, D_MODEL), z2, h2,
                    lay['wa'], lay['wl'], lay['wo'], lay['bga'], lay['bgr'])
        if 'router' in lay:
            h2 = _moe(h2, lay['norm_ffn'], lay['router'], lay['wg'], lay['wu'], lay['wd'])
        else:
            h2 = _ffn(h2, lay['norm_ffn'], lay['wg'], lay['wu'], lay['wd'])
    return h2.reshape(b, lp, D_MODEL)[:, N_META:seq_len]


def kernel(x_prompt, x_sample, meta_tokens, norm_mix, norm_ffn, w_in, b_gate, q_norm, k_norm, lambda_q1, lambda_k1, lambda_q2, lambda_k2, attn_subln, w_attn_branch, conv_w, conv_b, lru_wa, lru_ba, lru_wx, lru_bx, lru_lambda, w_lru_branch, w_out, ffn_w_gate, ffn_w_up, ffn_w_down, moe_router, moe_w_gate, moe_w_up, moe_w_down):
    p = dict(norm_mix=norm_mix, norm_ffn=norm_ffn, w_in=w_in, b_gate=b_gate, q_norm=q_norm,
             k_norm=k_norm, lambda_q1=lambda_q1, lambda_k1=lambda_k1, lambda_q2=lambda_q2,
             lambda_k2=lambda_k2, attn_subln=attn_subln, w_attn_branch=w_attn_branch,
             conv_w=conv_w, conv_b=conv_b, lru_wa=lru_wa, lru_ba=lru_ba, lru_wx=lru_wx,
             lru_bx=lru_bx, lru_lambda=lru_lambda, w_lru_branch=w_lru_branch, w_out=w_out,
             ffn_w_gate=ffn_w_gate, ffn_w_up=ffn_w_up, ffn_w_down=ffn_w_down,
             moe_router=moe_router, moe_w_gate=moe_w_gate, moe_w_up=moe_w_up,
             moe_w_down=moe_w_down)
    layers = _prepare(p)
    return (_trunk(x_prompt, meta_tokens, layers), _trunk(x_sample, meta_tokens, layers))
```

```python
import functools
import math

import jax
import jax.numpy as jnp
from jax import lax
from jax.experimental import pallas as pl
from jax.experimental.pallas import tpu as pltpu

F32 = jnp.float32
BF16 = jnp.bfloat16

D_MODEL = 1024
N_META = 16
N_HEADS = 8
HEAD_DIM = 64
V_DIM = 2 * HEAD_DIM
LRU_BLOCKS = 16
LRU_BLOCK_W = D_MODEL // LRU_BLOCKS
LRU_C = 8.0
N_EXPERTS = 8
EPS = 1e-6
IN_WIDTH = 7 * D_MODEL

LANE = 128
SUBLANE = 8
VMEM_LIMIT_BYTES = 48 * 1024 * 1024
NEG_BIG = -1e30

ATT_TILE = 512
ATT_MAX_UNROLL = 8
EXP_ZERO_LOGIT = 105.0
VT_ROWS = V_DIM + 16
ATT_TAIL = LANE
ATT_VMEM_LIMIT_BYTES = 56 * 1024 * 1024
MASK_LOGIT = -30000.0
SAFE_LOGIT_SPAN = 80.0

TOP_K = 2
MOE_TM = 512
MOE_TF = 512
MOE_DMA_UNROLL = 8

LRU_CB = 128
LRU_TT = 128


def _largest_tile(n, target, mult=LANE):
    assert n % mult == 0, (n, mult)
    best = mult
    t = mult
    while t <= min(n, target):
        if n % t == 0:
            best = t
        t += mult
    return best


def _cparams(*sem):
    return pltpu.CompilerParams(dimension_semantics=sem, vmem_limit_bytes=VMEM_LIMIT_BYTES)


def _rmsnorm_rows(x, g):
    ms = jnp.mean(x * x, axis=-1, keepdims=True)
    return x * lax.rsqrt(ms + EPS) * g


def _in_proj_kernel(h_ref, g_ref, w_ref, z_ref, xn_ref):
    @pl.when(pl.program_id(1) == 0)
    def _():
        xn_ref[...] = _rmsnorm_rows(h_ref[...], g_ref[...]).astype(BF16)

    z_ref[...] = jnp.dot(xn_ref[...], w_ref[...], preferred_element_type=F32)


def _in_proj(h2, g, w):
    m = h2.shape[0]
    tm = _largest_tile(m, 1024)
    tn = _largest_tile(IN_WIDTH, 1792, 2 * LANE)
    return pl.pallas_call(
        _in_proj_kernel,
        grid=(m // tm, IN_WIDTH // tn),
        in_specs=[
            pl.BlockSpec((tm, D_MODEL), lambda i, j: (i, 0)),
            pl.BlockSpec((1, D_MODEL), lambda i, j: (0, 0)),
            pl.BlockSpec((D_MODEL, tn), lambda i, j: (0, j)),
        ],
        out_specs=pl.BlockSpec((tm, tn), lambda i, j: (i, j)),
        out_shape=jax.ShapeDtypeStruct((m, IN_WIDTH), F32),
        scratch_shapes=[pltpu.VMEM((tm, D_MODEL), BF16)],
        compiler_params=_cparams("parallel", "arbitrary"),
        name="in_proj",
    )(h2, g, w)


def _attn_kernel(sc_ref, q_ref, k_ref, v_ref, qg_ref, kg_ref, lam_ref, sg_ref, o_ref,
                 kb_ref, vb_ref, m_ref, l_ref, acc_ref, *, seq_len, tq, tk):
    lp = k_ref.shape[0]
    nk = lp // tk
    head = pl.program_id(1)
    qi = pl.program_id(2)
    lane = lax.broadcasted_iota(jnp.int32, (1, LANE), 1)
    lo = lane < HEAD_DIM

    def half_rmsnorm(x, g):
        sq = x * x
        s_lo = jnp.sum(jnp.where(lo, sq, 0.0), axis=-1, keepdims=True)
        s_hi = jnp.sum(jnp.where(lo, 0.0, sq), axis=-1, keepdims=True)
        ms = jnp.where(lo, s_lo, s_hi) * (1.0 / HEAD_DIM)
        return x * lax.rsqrt(ms + EPS) * g

    @pl.when(qi == 0)
    def _():
        def fill(c, carry):
            r = pl.ds(pl.multiple_of(c * tk, tk), tk)
            kb_ref[r, :] = half_rmsnorm(k_ref[r, :], kg_ref[...]).astype(BF16)
            vb_ref[r, :] = v_ref[r, :].astype(BF16)
            return carry
        lax.fori_loop(0, nk, fill, 0)

    qn = half_rmsnorm(q_ref[...], qg_ref[...]) * (1.0 / math.sqrt(HEAD_DIM))
    q_comp = (jnp.where(lo, qn, 0.0).astype(BF16), jnp.where(lo, 0.0, qn).astype(BF16))

    m_ref[...] = jnp.full(m_ref.shape, NEG_BIG, F32)
    l_ref[...] = jnp.zeros(l_ref.shape, F32)
    acc_ref[...] = jnp.zeros(acc_ref.shape, F32)

    slope = sc_ref[head]
    rel = (lax.broadcasted_iota(jnp.int32, (tq, tk), 0)
           - lax.broadcasted_iota(jnp.int32, (tq, tk), 1)).astype(F32)
    col = lax.broadcasted_iota(jnp.int32, (1, tk), 1)

    def chunk(kc, carry):
        k0 = pl.multiple_of(kc * tk, tk)
        kblk = kb_ref[pl.ds(k0, tk), :]
        vblk = vb_ref[pl.ds(k0, tk), :]
        off = (qi * tq - k0).astype(F32)
        bias = -slope * jnp.abs(rel + off) + jnp.where(col + k0 < seq_len, 0.0, NEG_BIG)
        for c in range(2):
            s = lax.dot_general(q_comp[c], kblk, (((1,), (1,)), ((), ())),
                                preferred_element_type=F32) + bias
            m_prev = m_ref[c]
            m_new = jnp.maximum(m_prev, jnp.max(s, axis=-1, keepdims=True))
            alpha = jnp.exp(m_prev - m_new)
            p = jnp.exp(s - m_new)
            l_ref[c] = alpha * l_ref[c] + jnp.sum(p, axis=-1, keepdims=True)
            acc_ref[c] = alpha * acc_ref[c] + jnp.dot(p.astype(BF16), vblk,
                                                      preferred_element_type=F32)
            m_ref[c] = m_new
        return carry

    lax.fori_loop(0, nk, chunk, 0)

    lam_init = sc_ref[N_HEADS]
    lp_ = lam_ref[...]
    lam = (jnp.exp(jnp.sum(lp_[0:1] * lp_[1:2], axis=-1, keepdims=True))
           - jnp.exp(jnp.sum(lp_[2:3] * lp_[3:4], axis=-1, keepdims=True)) + lam_init)
    o = acc_ref[0] / l_ref[0] - lam * (acc_ref[1] / l_ref[1])
    o = _rmsnorm_rows(o, sg_ref[...]) * (1.0 - lam_init)
    o_ref[...] = o.astype(BF16)


def _attn_bounded_kernel(sc_ref, q_ref, k_ref, v_ref, qg_ref, kg_ref, lam_ref, sg_ref, prev_ref,
                         o_ref, kq_ref, vt_ref, vtail_ref, lhs_ref, bias_ref, acc_ref,
                         *, seq_len, t, head0, reach):
    lp = k_ref.shape[0]
    s_main = lp - ATT_TAIL
    n_main = s_main // t
    del prev_ref
    slope = sc_ref[head0 + pl.program_id(1)]
    lam_init = sc_ref[N_HEADS]
    lane = lax.broadcasted_iota(jnp.int32, (1, LANE), 1)
    lo = lane < HEAD_DIM
    nt = (((1,), (1,)), ((), ()))

    def half_rmsnorm(x, g):
        sq = x * x
        s_lo = jnp.sum(jnp.where(lo, sq, 0.0), axis=-1, keepdims=True)
        s_hi = jnp.sum(jnp.where(lo, 0.0, sq), axis=-1, keepdims=True)
        ms = jnp.where(lo, s_lo, s_hi) * (1.0 / HEAD_DIM)
        return x * lax.rsqrt(ms + EPS) * g

    def split2(x):
        hi = x.astype(BF16).astype(F32)
        return hi, x - hi

    def pos_col(r0, n):
        return lax.broadcasted_iota(jnp.int32, (n, 1), 0) + r0

    lp_ = lam_ref[...]
    lam = (jnp.exp(jnp.sum(lp_[0:1] * lp_[1:2], axis=-1, keepdims=True))
           - jnp.exp(jnp.sum(lp_[2:3] * lp_[3:4], axis=-1, keepdims=True)) + lam_init)

    ones_row = jnp.where(lax.broadcasted_iota(jnp.int32, (VT_ROWS - V_DIM, 1), 0) == 0, 1.0, 0.0)

    def fill(r0, n, vt_dst):
        rows = pos_col(r0, n)
        valid = rows < seq_len
        kn = jnp.where(valid, half_rmsnorm(k_ref[pl.ds(r0, n), :], kg_ref[...]), 0.0)
        sj_hi, sj_lo = split2(slope * rows.astype(F32))
        aug = jnp.where(lane == 0, -1.0,
              jnp.where((lane == 1) | (lane == 2), 1.0,
              jnp.where(lane == 3, sj_hi,
              jnp.where(lane == 4, sj_lo,
              jnp.where(lane == 5, jnp.where(valid, 0.0, MASK_LOGIT), 0.0)))))
        kq_ref[pl.ds(r0, n), 0:LANE] = kn.astype(BF16)
        kq_ref[pl.ds(r0, n), LANE:2 * LANE] = aug.astype(BF16)
        vt = jnp.transpose(jnp.where(valid, v_ref[pl.ds(r0, n), :], 0.0))
        vt_dst[...] = jnp.concatenate(
            [vt, jnp.broadcast_to(ones_row, (VT_ROWS - V_DIM, n))], axis=0).astype(BF16)

    def fill_chunk(c, carry):
        fill(pl.multiple_of(c * t, t), t, vt_ref.at[c])
        return carry
    lax.fori_loop(0, n_main, fill_chunk, 0)
    fill(s_main, ATT_TAIL, vtail_ref)

    def comp_max(g, c):
        return jnp.max(jnp.where(lo == (c == 0), jnp.abs(g), 0.0), axis=-1, keepdims=True)
    bounds = [math.sqrt(HEAD_DIM) * comp_max(qg_ref[...], c) * comp_max(kg_ref[...], c)
              for c in range(2)]

    rel = (lax.broadcasted_iota(jnp.int32, (t, t), 0)
           - lax.broadcasted_iota(jnp.int32, (t, t), 1)).astype(F32)
    bias_ref[0] = jnp.zeros((t, t), F32)
    bias_ref[1] = -slope * jnp.abs(rel)

    def build_lhs(r0, n):
        rows = pos_col(r0, n)
        qn = half_rmsnorm(q_ref[pl.ds(r0, n), :], qg_ref[...]) * (1.0 / math.sqrt(HEAD_DIM))
        si_hi, si_lo = split2(slope * rows.astype(F32))
        side = jnp.where(lane == 1, si_hi,
               jnp.where(lane == 2, si_lo,
               jnp.where((lane == 3) | (lane == 4), -1.0, 0.0)))
        for c in range(2):
            qc = jnp.where(lo, qn, 0.0) if c == 0 else jnp.where(lo, 0.0, qn)
            base = jnp.broadcast_to(
                jnp.where(lane == 0, bounds[c], jnp.where(lane == 5, 1.0, 0.0)), (n, LANE))
            qcb = qc.astype(BF16)
            for x, aug in enumerate((base - side, base, base + side)):
                lhs_ref[x, c * n:(c + 1) * n, 0:LANE] = qcb
                lhs_ref[x, c * n:(c + 1) * n, LANE:2 * LANE] = aug.astype(BF16)

    def process(x, n, k0, nk, vt, bias):
        s = lax.dot_general(kq_ref[pl.ds(k0, nk), :], lhs_ref[x, 0:2 * n, :], nt,
                            preferred_element_type=F32)
        if bias is not None:
            s = s + jnp.concatenate([bias, bias], axis=1)
        p = jnp.exp(s).astype(BF16)
        acc_ref[:, 0:2 * n] += jnp.dot(vt, p, preferred_element_type=F32)

    def finalize(r0, n):
        a0 = acc_ref[:, 0:n]
        a1 = acc_ref[:, n:2 * n]
        o = (a0[0:V_DIM] * (1.0 / a0[V_DIM:V_DIM + 1])
             - (lam * (1.0 / a1[V_DIM:V_DIM + 1])) * a1[0:V_DIM])
        ms = jnp.mean(o * o, axis=0, keepdims=True)
        o = o * lax.rsqrt(ms + EPS) * sg_ref[...] * (1.0 - lam_init)
        pos = lax.broadcasted_iota(jnp.int32, (1, n), 1) + r0
        o = jnp.where(pos < seq_len, o, 0.0)
        o_ref[pl.ds(r0, n), :] = jnp.transpose(o).astype(BF16)

    window = min(2 * reach + 1, n_main)

    def full_chunks(qi, n, kc0, count):
        def body(w, carry):
            kc = kc0 + w
            x = jnp.where(kc < qi, 0, jnp.where(kc == qi, 1, 2))
            d = jnp.where(kc == qi, 1, 0)
            process(x, n, pl.multiple_of(kc * t, t), t, vt_ref[kc], bias_ref[d, :, 0:n])
            return carry
        lax.fori_loop(0, count, body, 0, unroll=min(count, ATT_MAX_UNROLL))

    def q_main(qi, carry):
        r0 = pl.multiple_of(qi * t, t)
        build_lhs(r0, t)
        acc_ref[...] = jnp.zeros(acc_ref.shape, F32)
        full_chunks(qi, t, jnp.clip(qi - reach, 0, n_main - window), window)

        @pl.when(n_main - qi <= reach)
        def _():
            process(2, t, s_main, ATT_TAIL, vtail_ref[...], None)
        finalize(r0, t)
        return carry
    lax.fori_loop(0, n_main, q_main, 0)

    build_lhs(s_main, ATT_TAIL)
    acc_ref[...] = jnp.zeros(acc_ref.shape, F32)
    tail_count = min(reach, n_main)
    full_chunks(n_main, ATT_TAIL, n_main - tail_count, tail_count)
    process(1, ATT_TAIL, s_main, ATT_TAIL, vtail_ref[...], bias_ref[1, 0:ATT_TAIL, 0:ATT_TAIL])
    finalize(s_main, ATT_TAIL)


def _chunk_reach(head, t):
    slope = 2.0 ** (-8.0 * (head + 1) / N_HEADS)
    return max(1, math.ceil((EXP_ZERO_LOGIT / slope - 1.0) / t))


def _attention_bounded(z3, scal, qg, kg, lamp, sg, seq_len):
    b, lp, _ = z3.shape
    t = _largest_tile(lp - ATT_TAIL, ATT_TILE)
    n_main = (lp - ATT_TAIL) // t
    groups = []
    for head in range(N_HEADS):
        reach = min(_chunk_reach(head, t), n_main)
        if groups and min(2 * groups[-1][2] + 1, n_main) == min(2 * reach + 1, n_main):
            groups[-1][1] += 1
            groups[-1][2] = max(groups[-1][2], reach)
        else:
            groups.append([head, 1, reach])
    small = lambda shape: pl.BlockSpec(shape, lambda bi, hi: (0, 0))
    att = jnp.zeros((b, lp, N_HEADS * V_DIM), BF16)
    for head0, count, reach in groups:
        kern = functools.partial(_attn_bounded_kernel, seq_len=seq_len, t=t, head0=head0,
                                 reach=reach)
        col = lambda off, head0=head0: pl.BlockSpec(
            (None, lp, V_DIM), lambda bi, hi: (bi, 0, off + head0 + hi))
        att = pl.pallas_call(
            kern,
            grid=(b, count),
            in_specs=[
                pl.BlockSpec(memory_space=pltpu.SMEM),
                col(0), col(N_HEADS), col(2 * N_HEADS),
                small((1, V_DIM)), small((1, V_DIM)), small((4, HEAD_DIM)), small((V_DIM, 1)),
                pl.BlockSpec(memory_space=pl.ANY),
            ],
            out_specs=col(0),
            out_shape=jax.ShapeDtypeStruct(att.shape, att.dtype),
            input_output_aliases={8: 0},
            scratch_shapes=[
                pltpu.VMEM((lp, 2 * LANE), BF16),
                pltpu.VMEM((n_main, VT_ROWS, t), BF16), pltpu.VMEM((VT_ROWS, ATT_TAIL), BF16),
                pltpu.VMEM((3, 2 * t, 2 * LANE), BF16),
                pltpu.VMEM((2, t, t), F32),
                pltpu.VMEM((VT_ROWS, 2 * t), F32),
            ],
            compiler_params=pltpu.CompilerParams(dimension_semantics=("parallel", "parallel"),
                                                 vmem_limit_bytes=ATT_VMEM_LIMIT_BYTES),
            name="diff_attention_bounded",
        )(scal, z3, z3, z3, qg, kg, lamp, sg.reshape(V_DIM, 1), att)
    return att


def _attention(z3, scal, qg, kg, lamp, sg, seq_len):
    span = 2.0 * HEAD_DIM / math.sqrt(HEAD_DIM) * jnp.max(jnp.abs(qg)) * jnp.max(jnp.abs(kg))
    args = (z3, scal, qg, kg, lamp, sg)
    return lax.cond(span <= SAFE_LOGIT_SPAN,
                    lambda *a: _attention_bounded(*a, seq_len),
                    lambda *a: _attention_online(*a, seq_len), *args)


def _attention_online(z3, scal, qg, kg, lamp, sg, seq_len):
    b, lp, _ = z3.shape
    tq = _largest_tile(lp, 768)
    tk = tq
    kern = functools.partial(_attn_kernel, seq_len=seq_len, tq=tq, tk=tk)
    small = lambda shape: pl.BlockSpec(shape, lambda bi, hi, qi: (0, 0))
    return pl.pallas_call(
        kern,
        grid=(b, N_HEADS, lp // tq),
        in_specs=[
            pl.BlockSpec(memory_space=pltpu.SMEM),
            pl.BlockSpec((None, tq, V_DIM), lambda bi, hi, qi: (bi, qi, hi)),
            pl.BlockSpec((None, lp, V_DIM), lambda bi, hi, qi: (bi, 0, N_HEADS + hi)),
            pl.BlockSpec((None, lp, V_DIM), lambda bi, hi, qi: (bi, 0, 2 * N_HEADS + hi)),
            small((1, V_DIM)), small((1, V_DIM)), small((4, HEAD_DIM)), small((1, V_DIM)),
        ],
        out_specs=pl.BlockSpec((None, tq, V_DIM), lambda bi, hi, qi: (bi, qi, hi)),
        out_shape=jax.ShapeDtypeStruct((b, lp, N_HEADS * V_DIM), BF16),
        scratch_shapes=[
            pltpu.VMEM((lp, V_DIM), BF16), pltpu.VMEM((lp, V_DIM), BF16),
            pltpu.VMEM((2, tq, 1), F32), pltpu.VMEM((2, tq, 1), F32),
            pltpu.VMEM((2, tq, V_DIM), F32),
        ],
        compiler_params=_cparams("parallel", "parallel", "arbitrary"),
        name="diff_attention",
    )(scal, z3, z3, z3, qg, kg, lamp, sg)


def _lru_kernel(xl_ref, gl_ref, cw_ref, cbias_ref, w_ref, ba_ref, bx_ref, lam_ref, o_ref,
                xp_ref, hf_ref, hb_ref, *, seq_len, tt):
    lp, cb = xl_ref.shape
    n = lp // tt
    pad = 8
    rows_t = lax.broadcasted_iota(jnp.int32, (tt, 1), 0)

    xp_ref[0:pad, :] = jnp.zeros((pad, cb), F32)
    xp_ref[lp + pad:lp + 2 * pad, :] = jnp.zeros((pad, cb), F32)

    def copy_in(c, carry):
        r0 = pl.multiple_of(c * tt, tt)
        x = xl_ref[pl.ds(r0, tt), :]
        xp_ref[pl.ds(r0 + pad, tt), :] = jnp.where(rows_t + r0 < seq_len, x, 0.0)
        return carry
    lax.fori_loop(0, n, copy_in, 0)

    cw = cw_ref[...]

    def gates(c, d):
        r0 = pl.multiple_of(c * tt, tt)
        win = xp_ref[pl.ds(r0, tt + 2 * pad), :]
        wn = tt + 2 * pad
        xc = (pltpu.roll(win, 2, 0)[pad:pad + tt] * cw[0:1]
              + pltpu.roll(win, 1, 0)[pad:pad + tt] * cw[1:2]
              + win[pad:pad + tt] * cw[2:3]
              + pltpu.roll(win, wn - 1, 0)[pad:pad + tt] * cw[3:4]
              + cbias_ref[...])
        pre = jnp.dot(xc.astype(BF16), w_ref[:, d * 2 * cb:(d + 1) * 2 * cb],
                      preferred_element_type=F32)
        r = jax.nn.sigmoid(pre[:, :cb] + ba_ref[d:d + 1, :])
        i = jax.nn.sigmoid(pre[:, cb:] + bx_ref[d:d + 1, :])
        nl = -lam_ref[d:d + 1, :]
        softplus = jnp.maximum(nl, 0.0) + jnp.log1p(jnp.exp(-jnp.abs(nl)))
        log_a = (-LRU_C * r) * softplus
        a = jnp.exp(log_a)
        th = jnp.tanh(log_a)
        m2 = -2.0 * th / (1.0 - th)
        mult = jnp.where(m2 > 0.0, m2 * lax.rsqrt(m2), 0.0)
        u = mult * (i * xc)
        u = jnp.where(rows_t + r0 < seq_len, u, 0.0)
        return a, u

    ng = tt // SUBLANE
    sub = lax.broadcasted_iota(jnp.int32, (1, SUBLANE, 1), 1)

    def scan_chunk(a, u, carry, reverse):
        a3 = a.reshape(ng, SUBLANE, cb)
        u3 = u.reshape(ng, SUBLANE, cb)
        d = 1
        while d < SUBLANE:
            if reverse:
                shift, keep = SUBLANE - d, sub < SUBLANE - d
            else:
                shift, keep = d, sub >= d
            u3 = u3 + a3 * jnp.where(keep, pltpu.roll(u3, shift, 1), 0.0)
            a3 = a3 * jnp.where(keep, pltpu.roll(a3, shift, 1), 1.0)
            d *= 2
        hs = [None] * ng
        for g in (range(ng - 1, -1, -1) if reverse else range(ng)):
            hg = u3[g] + a3[g] * carry
            hs[g] = hg
            carry = hg[0:1, :] if reverse else hg[SUBLANE - 1:SUBLANE, :]
        return jnp.concatenate(hs, axis=0), carry

    def both(j, carry):
        cf = j
        h, state_f = scan_chunk(*gates(cf, 0), carry[0], False)
        hf_ref[pl.ds(pl.multiple_of(cf * tt, tt), tt), :] = h
        cr = n - 1 - j
        h, state_b = scan_chunk(*gates(cr, 1), carry[1], True)
        hb_ref[pl.ds(pl.multiple_of(cr * tt, tt), tt), :] = h
        return state_f, state_b
    zero_state = jnp.zeros((1, cb), F32)
    lax.fori_loop(0, n, both, (zero_state, zero_state))

    def combine(c, carry):
        r = pl.ds(pl.multiple_of(c * tt, tt), tt)
        o_ref[r, :] = ((hf_ref[r, :] + hb_ref[r, :]) * jax.nn.gelu(gl_ref[r, :])).astype(BF16)
        return carry
    lax.fori_loop(0, n, combine, 0)


def _lru(z3, cw, cbias, wcat, ba, bx, lam, seq_len):
    b, lp, _ = z3.shape
    cb = LRU_CB
    nc = D_MODEL // cb
    kern = functools.partial(_lru_kernel, seq_len=seq_len, tt=LRU_TT)
    chan = lambda rows: pl.BlockSpec((rows, cb), lambda bi, ci: (0, ci))
    return pl.pallas_call(
        kern,
        grid=(b, nc),
        in_specs=[
            pl.BlockSpec((None, lp, cb), lambda bi, ci: (bi, 0, 3 * nc + ci)),
            pl.BlockSpec((None, lp, cb), lambda bi, ci: (bi, 0, 4 * nc + ci)),
            chan(4), chan(1),
            pl.BlockSpec((None, cb, 4 * cb), lambda bi, ci: (ci, 0, 0)),
            chan(2), chan(2), chan(2),
        ],
        out_specs=pl.BlockSpec((None, lp, cb), lambda bi, ci: (bi, 0, ci)),
        out_shape=jax.ShapeDtypeStruct((b, lp, D_MODEL), BF16),
        scratch_shapes=[pltpu.VMEM((lp + 16, cb), F32), pltpu.VMEM((lp, cb), F32),
                        pltpu.VMEM((lp, cb), F32)],
        compiler_params=_cparams("parallel", "parallel"),
        name="rglru",
    )(z3, z3, cw, cbias, wcat, ba, bx, lam)


def _merge_kernel(att_ref, yl_ref, ga_ref, gr_ref, h_ref, wa_ref, wl_ref, wo_ref,
                  bga_ref, bgr_ref, o_ref):
    ya = jnp.dot(att_ref[...], wa_ref[...], preferred_element_type=F32)
    yr = jnp.dot(yl_ref[...], wl_ref[...], preferred_element_type=F32)
    merged = (jax.nn.sigmoid(ga_ref[...] + bga_ref[...]) * ya
              + jax.nn.sigmoid(gr_ref[...] + bgr_ref[...]) * yr)
    o_ref[...] = h_ref[...] + jnp.dot(merged.astype(BF16), wo_ref[...],
                                      preferred_element_type=F32)


def _merge(att2, yl2, z2, h2, wa, wl, wo, bga, bgr):
    m = h2.shape[0]
    tm = _largest_tile(m, 256)
    row = lambda col: pl.BlockSpec((tm, D_MODEL), lambda i: (i, col))
    full = lambda shape: pl.BlockSpec(shape, lambda i: (0, 0))
    return pl.pallas_call(
        _merge_kernel,
        grid=(m // tm,),
        in_specs=[row(0), row(0), row(5), row(6), row(0),
                  full((D_MODEL, D_MODEL)), full((D_MODEL, D_MODEL)), full((D_MODEL, D_MODEL)),
                  full((1, D_MODEL)), full((1, D_MODEL))],
        out_specs=row(0),
        out_shape=jax.ShapeDtypeStruct((m, D_MODEL), F32),
        compiler_params=_cparams("parallel"),
        name="merge_out_proj",
    )(att2, yl2, z2, z2, h2, wa, wl, wo, bga, bgr)


def _ffn_kernel(*refs, chunks_per_expert):
    if chunks_per_expert:
        h_ref, g_ref, wr_ref, wg_ref, wu_ref, wd_ref, o_ref, hn_ref, acc_ref, gate_ref = refs
    else:
        h_ref, g_ref, wg_ref, wu_ref, wd_ref, o_ref, hn_ref, acc_ref = refs
    j = pl.program_id(1)

    @pl.when(j == 0)
    def _():
        h = h_ref[...]
        hn = _rmsnorm_rows(h, g_ref[...])
        hn_ref[...] = hn.astype(BF16)
        acc_ref[...] = h
        if chunks_per_expert:
            logits = jnp.dot(hn, wr_ref[...], preferred_element_type=F32,
                             precision=lax.Precision.HIGHEST)
            lane = lax.broadcasted_iota(jnp.int32, logits.shape, 1).astype(F32)
            lg = jnp.where(lane < N_EXPERTS, logits, NEG_BIG)
            m1 = jnp.max(lg, axis=-1, keepdims=True)
            i1 = jnp.min(jnp.where(lg == m1, lane, float(LANE)), axis=-1, keepdims=True)
            lg2 = jnp.where(lane == i1, NEG_BIG, lg)
            m2 = jnp.max(lg2, axis=-1, keepdims=True)
            i2 = jnp.min(jnp.where(lg2 == m2, lane, float(LANE)), axis=-1, keepdims=True)
            e2 = jnp.exp(m2 - m1)
            den = 1.0 + e2
            gate_ref[...] = jnp.where(lane == i1, 1.0 / den,
                                      jnp.where(lane == i2, e2 / den, 0.0))

    hn = hn_ref[...]
    gg = jnp.dot(hn, wg_ref[...], preferred_element_type=F32)
    uu = jnp.dot(hn, wu_ref[...], preferred_element_type=F32)
    act = (jax.nn.silu(gg) * uu).astype(BF16)
    y = jnp.dot(act, wd_ref[...], preferred_element_type=F32)
    if chunks_per_expert:
        e = (j // chunks_per_expert).astype(F32)
        gates = gate_ref[...]
        lane = lax.broadcasted_iota(jnp.int32, gates.shape, 1).astype(F32)
        y = y * jnp.sum(jnp.where(lane == e, gates, 0.0), axis=-1, keepdims=True)
    acc_ref[...] += y

    @pl.when(j == pl.num_programs(1) - 1)
    def _():
        o_ref[...] = acc_ref[...]


def _ffn(h2, g, wg, wu, wd, router=None, d_ff_expert=None):
    m = h2.shape[0]
    f = wg.shape[1]
    tm = _largest_tile(m, 1024)
    tf = 512
    in_specs = [pl.BlockSpec((tm, D_MODEL), lambda i, j: (i, 0)),
                pl.BlockSpec((1, D_MODEL), lambda i, j: (0, 0))]
    args = [h2, g]
    scratch = [pltpu.VMEM((tm, D_MODEL), BF16), pltpu.VMEM((tm, D_MODEL), F32)]
    cpe = 0
    if router is not None:
        assert d_ff_expert % tf == 0
        cpe = d_ff_expert // tf
        in_specs.append(pl.BlockSpec((D_MODEL, LANE), lambda i, j: (0, 0)))
        args.append(router)
        scratch.append(pltpu.VMEM((tm, LANE), F32))
    in_specs += [pl.BlockSpec((D_MODEL, tf), lambda i, j: (0, j)),
                 pl.BlockSpec((D_MODEL, tf), lambda i, j: (0, j)),
                 pl.BlockSpec((tf, D_MODEL), lambda i, j: (j, 0))]
    args += [wg, wu, wd]
    return pl.pallas_call(
        functools.partial(_ffn_kernel, chunks_per_expert=cpe),
        grid=(m // tm, f // tf),
        in_specs=in_specs,
        out_specs=pl.BlockSpec((tm, D_MODEL), lambda i, j: (i, 0)),
        out_shape=jax.ShapeDtypeStruct((m, D_MODEL), F32),
        scratch_shapes=scratch,
        compiler_params=_cparams("parallel", "arbitrary"),
        name="moe_ffn" if cpe else "dense_ffn",
    )(*args)


def _moe_router_kernel(h_ref, g_ref, wr_ref, r_ref, cnt_ref, tri_ref, base_ref):
    tm = h_ref.shape[0]

    @pl.when(pl.program_id(0) == 0)
    def _():
        row = lax.broadcasted_iota(jnp.int32, (tm, tm), 0)
        col = lax.broadcasted_iota(jnp.int32, (tm, tm), 1)
        tri_ref[...] = jnp.where(col < row, 1.0, 0.0).astype(BF16)
        base_ref[...] = jnp.zeros(base_ref.shape, F32)

    hn = _rmsnorm_rows(h_ref[...], g_ref[...])
    logits = jnp.dot(hn, wr_ref[...], preferred_element_type=F32, precision=lax.Precision.HIGHEST)
    lane = lax.broadcasted_iota(jnp.int32, logits.shape, 1).astype(F32)
    lg = jnp.where(lane < N_EXPERTS, logits, NEG_BIG)
    m1 = jnp.max(lg, axis=-1, keepdims=True)
    i1 = jnp.min(jnp.where(lg == m1, lane, float(LANE)), axis=-1, keepdims=True)
    lg2 = jnp.where(lane == i1, NEG_BIG, lg)
    m2 = jnp.max(lg2, axis=-1, keepdims=True)
    i2 = jnp.min(jnp.where(lg2 == m2, lane, float(LANE)), axis=-1, keepdims=True)
    e2 = jnp.exp(m2 - m1)
    den = 1.0 + e2
    oh1 = lane == i1
    oh2 = lane == i2
    assigned = jnp.where(oh1 | oh2, 1.0, 0.0)
    before = jnp.dot(tri_ref[...], assigned.astype(BF16), preferred_element_type=F32) + base_ref[...]
    rank1 = jnp.sum(jnp.where(oh1, before, 0.0), axis=-1, keepdims=True)
    rank2 = jnp.sum(jnp.where(oh2, before, 0.0), axis=-1, keepdims=True)
    r_ref[...] = jnp.where(lane == 0, i1, jnp.where(lane == 1, i2,
                 jnp.where(lane == 2, 1.0 / den, jnp.where(lane == 3, e2 / den,
                 jnp.where(lane == 4, rank1, jnp.where(lane == 5, rank2, 0.0))))))
    base_ref[...] += jnp.sum(assigned, axis=0, keepdims=True)
    cnt_ref[...] = base_ref[...]


def _moe_router(h2, g, router):
    m = h2.shape[0]
    tm = _largest_tile(m, 1024)
    return pl.pallas_call(
        _moe_router_kernel,
        grid=(m // tm,),
        in_specs=[pl.BlockSpec((tm, D_MODEL), lambda i: (i, 0)),
                  pl.BlockSpec((1, D_MODEL), lambda i: (0, 0)),
                  pl.BlockSpec((D_MODEL, LANE), lambda i: (0, 0))],
        out_specs=[pl.BlockSpec((tm, LANE), lambda i: (i, 0)),
                   pl.BlockSpec((1, LANE), lambda i: (0, 0))],
        out_shape=[jax.ShapeDtypeStruct((m, LANE), F32), jax.ShapeDtypeStruct((1, LANE), F32)],
        scratch_shapes=[pltpu.VMEM((tm, tm), BF16), pltpu.VMEM((1, LANE), F32)],
        compiler_params=_cparams("arbitrary"),
        name="moe_router",
    )(h2, g, router)


def _slab_copy(src_ref, src_tok, dst_ref, dst_tok, sem):
    return pltpu.make_async_copy(
        src_ref.at[pl.ds(pl.multiple_of(src_tok * SUBLANE, SUBLANE), SUBLANE), :],
        dst_ref.at[pl.ds(pl.multiple_of(dst_tok * SUBLANE, SUBLANE), SUBLANE), :], sem)


def _rows_to_slabs(x, slab_ref):
    tm = x.shape[0]
    for s in range(D_MODEL // LANE):
        slab_ref[pl.ds(s, tm, stride=SUBLANE), :] = x[:, s * LANE:(s + 1) * LANE]


def _slabs_lane_block(slab_ref, s, tm):
    return slab_ref[pl.ds(s, tm, stride=SUBLANE), :]


def _moe_scatter_kernel(pos_ref, h_ref, g_ref, xs_in_ref, xs_ref, slab_ref, sem):
    del xs_in_ref
    tm = h_ref.shape[0]
    _rows_to_slabs(_rmsnorm_rows(h_ref[...], g_ref[...]), slab_ref)

    def body(r, carry):
        for s in range(TOP_K):
            _slab_copy(slab_ref, r, xs_ref, pos_ref[r, s], sem).start(priority=s)
        return carry
    lax.fori_loop(0, tm, body, 0, unroll=MOE_DMA_UNROLL)
    for s in range(TOP_K):
        pltpu.make_async_copy(slab_ref, xs_ref.at[pl.ds(0, tm * SUBLANE), :], sem).wait()


def _moe_scatter(pos, h2, g, n_rows):
    m = h2.shape[0]
    tm = _largest_tile(m, MOE_TM)
    xs0 = jnp.zeros((n_rows * SUBLANE, LANE), F32)
    return pl.pallas_call(
        _moe_scatter_kernel,
        grid=(m // tm,),
        in_specs=[pl.BlockSpec((tm, TOP_K), lambda i: (i, 0), memory_space=pltpu.SMEM),
                  pl.BlockSpec((tm, D_MODEL), lambda i: (i, 0)),
                  pl.BlockSpec((1, D_MODEL), lambda i: (0, 0)),
                  pl.BlockSpec(memory_space=pl.ANY)],
        out_specs=pl.BlockSpec(memory_space=pl.ANY),
        out_shape=jax.ShapeDtypeStruct(xs0.shape, xs0.dtype),
        input_output_aliases={3: 0},
        scratch_shapes=[pltpu.VMEM((tm * SUBLANE, LANE), F32), pltpu.SemaphoreType.DMA(())],
        compiler_params=_cparams("arbitrary"),
        name="moe_scatter",
    )(pos, h2, g, xs0)


def _moe_expert_kernel(te_ref, x_ref, wg_ref, wu_ref, wd_ref, y_ref, xb_ref, acc_ref):
    del te_ref
    j = pl.program_id(1)
    tm = xb_ref.shape[0]

    @pl.when(j == 0)
    def _():
        for s in range(D_MODEL // LANE):
            xb_ref[:, s * LANE:(s + 1) * LANE] = _slabs_lane_block(x_ref, s, tm).astype(BF16)
        acc_ref[...] = jnp.zeros(acc_ref.shape, F32)

    xb = xb_ref[...]
    gg = jnp.dot(xb, wg_ref[...], preferred_element_type=F32)
    uu = jnp.dot(xb, wu_ref[...], preferred_element_type=F32)
    act = (jax.nn.silu(gg) * uu).astype(BF16)
    acc_ref[...] += jnp.dot(act, wd_ref[...], preferred_element_type=F32)

    @pl.when(j == pl.num_programs(1) - 1)
    def _():
        _rows_to_slabs(acc_ref[...], y_ref)


def _moe_experts(tile_expert, xs, wg, wu, wd):
    n_rows = xs.shape[0] // SUBLANE
    fe = wg.shape[2]
    tm, tf = MOE_TM, MOE_TF
    slab_tile = pl.BlockSpec((tm * SUBLANE, LANE), lambda i, j, te: (i, 0))
    grid_spec = pltpu.PrefetchScalarGridSpec(
        num_scalar_prefetch=1,
        grid=(n_rows // tm, fe // tf),
        in_specs=[slab_tile,
                  pl.BlockSpec((None, D_MODEL, tf), lambda i, j, te: (te[i], 0, j)),
                  pl.BlockSpec((None, D_MODEL, tf), lambda i, j, te: (te[i], 0, j)),
                  pl.BlockSpec((None, tf, D_MODEL), lambda i, j, te: (te[i], j, 0))],
        out_specs=slab_tile,
        scratch_shapes=[pltpu.VMEM((tm, D_MODEL), BF16), pltpu.VMEM((tm, D_MODEL), F32)])
    return pl.pallas_call(
        _moe_expert_kernel,
        grid_spec=grid_spec,
        out_shape=jax.ShapeDtypeStruct(xs.shape, F32),
        compiler_params=_cparams("parallel", "arbitrary"),
        name="moe_experts",
    )(tile_expert, xs, wg, wu, wd)


def _moe_combine_kernel(pos_ref, r_ref, h_ref, ys_ref, o_ref, y0_ref, y1_ref, sem):
    tm = h_ref.shape[0]
    bufs = (y0_ref, y1_ref)

    def body(r, carry):
        for s in range(TOP_K):
            _slab_copy(ys_ref, pos_ref[r, s], bufs[s], r, sem).start(priority=s)
        return carry
    lax.fori_loop(0, tm, body, 0, unroll=MOE_DMA_UNROLL)
    for s in range(TOP_K):
        pltpu.make_async_copy(ys_ref.at[pl.ds(0, tm * SUBLANE), :], bufs[s], sem).wait()
    r = r_ref[...]
    for s in range(D_MODEL // LANE):
        o_ref[:, s * LANE:(s + 1) * LANE] = h_ref[:, s * LANE:(s + 1) * LANE] + (
            r[:, 2:3] * _slabs_lane_block(y0_ref, s, tm) + r[:, 3:4] * _slabs_lane_block(y1_ref, s, tm))


def _moe_combine(pos, route, h2, ys):
    m = h2.shape[0]
    tm = _largest_tile(m, MOE_TM)
    return pl.pallas_call(
        _moe_combine_kernel,
        grid=(m // tm,),
        in_specs=[pl.BlockSpec((tm, TOP_K), lambda i: (i, 0), memory_space=pltpu.SMEM),
                  pl.BlockSpec((tm, LANE), lambda i: (i, 0)),
                  pl.BlockSpec((tm, D_MODEL), lambda i: (i, 0)),
                  pl.BlockSpec(memory_space=pl.ANY)],
        out_specs=pl.BlockSpec((tm, D_MODEL), lambda i: (i, 0)),
        out_shape=jax.ShapeDtypeStruct((m, D_MODEL), F32),
        scratch_shapes=[pltpu.VMEM((tm * SUBLANE, LANE), F32), pltpu.VMEM((tm * SUBLANE, LANE), F32),
                        pltpu.SemaphoreType.DMA(())],
        compiler_params=_cparams("arbitrary"),
        name="moe_combine",
    )(pos, route, h2, ys)


def _moe(h2, g, router, wg, wu, wd):
    m = h2.shape[0]
    route, counts = _moe_router(h2, g, router)
    cnt = counts[0, :N_EXPERTS].astype(jnp.int32)
    padded = (cnt + MOE_TM - 1) // MOE_TM * MOE_TM
    ends = jnp.cumsum(padded)
    starts = ends - padded
    expert = route[:, 0:TOP_K].astype(jnp.int32)
    pos = starts[expert] + route[:, 4:4 + TOP_K].astype(jnp.int32)
    n_rows = TOP_K * m + N_EXPERTS * MOE_TM
    tile_start = jnp.arange(n_rows // MOE_TM, dtype=jnp.int32) * MOE_TM
    tile_expert = jnp.minimum(jnp.searchsorted(ends, tile_start, side='right'),
                              N_EXPERTS - 1).astype(jnp.int32)
    xs = _moe_scatter(pos, h2, g, n_rows)
    ys = _moe_experts(tile_expert, xs, wg, wu, wd)
    return _moe_combine(pos, route, h2, ys)


def _blockdiag_groups(w, per):
    g = LRU_BLOCKS // per
    w = w.reshape(g, per, LRU_BLOCK_W, LRU_BLOCK_W)
    eye = jnp.eye(per, dtype=w.dtype)
    out = jnp.einsum('gpio,pq->gpiqo', w, eye)
    return out.reshape(g, per * LRU_BLOCK_W, per * LRU_BLOCK_W)


def _prepare(p):
    depth = p['w_in'].shape[0]
    per = LRU_CB // LRU_BLOCK_W
    layers = []
    slopes = jnp.exp2(-8.0 * jnp.arange(1, N_HEADS + 1, dtype=F32) / N_HEADS)
    for l in range(depth):
        lam_init = 0.8 - 0.6 * math.exp(-0.3 * l)
        wcat = jnp.concatenate(
            [_blockdiag_groups(p['lru_wa'][l, 0], per), _blockdiag_groups(p['lru_wx'][l, 0], per),
             _blockdiag_groups(p['lru_wa'][l, 1], per), _blockdiag_groups(p['lru_wx'][l, 1], per)],
            axis=-1).astype(BF16)
        lay = dict(
            norm_mix=p['norm_mix'][l][None], norm_ffn=p['norm_ffn'][l][None],
            w_in=p['w_in'][l].astype(BF16),
            scal=jnp.concatenate([slopes, jnp.full((N_HEADS,), lam_init, F32)]),
            qg=jnp.tile(p['q_norm'][l], 2)[None], kg=jnp.tile(p['k_norm'][l], 2)[None],
            lamp=jnp.stack([p['lambda_q1'][l], p['lambda_k1'][l],
                            p['lambda_q2'][l], p['lambda_k2'][l]]),
            sg=p['attn_subln'][l][None],
            conv_w=p['conv_w'][l], conv_b=p['conv_b'][l][None], wcat=wcat,
            ba=p['lru_ba'][l], bx=p['lru_bx'][l], lam=p['lru_lambda'][l],
            wa=p['w_attn_branch'][l].astype(BF16), wl=p['w_lru_branch'][l].astype(BF16),
            wo=p['w_out'][l].astype(BF16),
            bga=p['b_gate'][l, :D_MODEL][None], bgr=p['b_gate'][l, D_MODEL:][None],
        )
        j = l // 2
        if l % 2 == 0:
            lay.update(wg=p['ffn_w_gate'][j].astype(BF16), wu=p['ffn_w_up'][j].astype(BF16),
                       wd=p['ffn_w_down'][j].astype(BF16))
        else:
            lay.update(
                wg=p['moe_w_gate'][j].astype(BF16), wu=p['moe_w_up'][j].astype(BF16),
                wd=p['moe_w_down'][j].astype(BF16),
                router=jnp.pad(p['moe_router'][j], ((0, 0), (0, LANE - N_EXPERTS))))
        layers.append(lay)
    return layers


def _trunk(x, meta, layers):
    b, s, _ = x.shape
    seq_len = s + N_META
    lp = -(-seq_len // LANE) * LANE
    h = jnp.concatenate(
        [jnp.broadcast_to(meta[None].astype(x.dtype), (b, N_META, D_MODEL)), x,
         jnp.zeros((b, lp - seq_len, D_MODEL), x.dtype)], axis=1)
    h2 = h.reshape(b * lp, D_MODEL)
    for lay in layers:
        z2 = _in_proj(h2, lay['norm_mix'], lay['w_in'])
        z3 = z2.reshape(b, lp, IN_WIDTH)
        att = _attention(z3, lay['scal'], lay['qg'], lay['kg'], lay['lamp'], lay['sg'], seq_len)
        yl = _lru(z3, lay['conv_w'], lay['conv_b'], lay['wcat'], lay['ba'], lay['bx'],
                  lay['lam'], seq_len)
        h2 = _merge(att.reshape(b * lp, D_MODEL), yl.reshape(b * lp, D_MODEL), z2, h2,
                    lay['wa'], lay['wl'], lay['wo'], lay['bga'], lay['bgr'])
        if 'router' in lay:
            h2 = _moe(h2, lay['norm_ffn'], lay['router'], lay['wg'], lay['wu'], lay['wd'])
        else:
            h2 = _ffn(h2, lay['norm_ffn'], lay['wg'], lay['wu'], lay['wd'])
    return h2.reshape(b, lp, D_MODEL)[:, N_META:seq_len]


def kernel(x_prompt, x_sample, meta_tokens, norm_mix, norm_ffn, w_in, b_gate, q_norm, k_norm, lambda_q1, lambda_k1, lambda_q2, lambda_k2, attn_subln, w_attn_branch, conv_w, conv_b, lru_wa, lru_ba, lru_wx, lru_bx, lru_lambda, w_lru_branch, w_out, ffn_w_gate, ffn_w_up, ffn_w_down, moe_router, moe_w_gate, moe_w_up, moe_w_down):
    p = dict(norm_mix=norm_mix, norm_ffn=norm_ffn, w_in=w_in, b_gate=b_gate, q_norm=q_norm,
             k_norm=k_norm, lambda_q1=lambda_q1, lambda_k1=lambda_k1, lambda_q2=lambda_q2,
             lambda_k2=lambda_k2, attn_subln=attn_subln, w_attn_branch=w_attn_branch,
             conv_w=conv_w, conv_b=conv_b, lru_wa=lru_wa, lru_ba=lru_ba, lru_wx=lru_wx,
             lru_bx=lru_bx, lru_lambda=lru_lambda, w_lru_branch=w_lru_branch, w_out=w_out,
             ffn_w_gate=ffn_w_gate, ffn_w_up=ffn_w_up, ffn_w_down=ffn_w_down,
             moe_router=moe_router, moe_w_gate=moe_w_gate, moe_w_up=moe_w_up,
             moe_w_down=moe_w_down)
    layers = _prepare(p)
    return (_trunk(x_prompt, meta_tokens, layers), _trunk(x_sample, meta_tokens, layers))
```

```python
import functools
import math

import jax
import jax.numpy as jnp
from jax import lax
from jax.experimental import pallas as pl
from jax.experimental.pallas import tpu as pltpu

F32 = jnp.float32
BF16 = jnp.bfloat16

D_MODEL = 1024
N_META = 16
N_HEADS = 8
HEAD_DIM = 64
V_DIM = 2 * HEAD_DIM
LRU_BLOCKS = 16
LRU_BLOCK_W = D_MODEL // LRU_BLOCKS
LRU_C = 8.0
N_EXPERTS = 8
EPS = 1e-6
IN_WIDTH = 7 * D_MODEL

LANE = 128
SUBLANE = 8
VMEM_LIMIT_BYTES = 48 * 1024 * 1024
NEG_BIG = -1e30

ATT_TILE = 512
ATT_MAX_UNROLL = 8
EXP_ZERO_LOGIT = 105.0
VT_ROWS = V_DIM + 16
ATT_TAIL = LANE
ATT_VMEM_LIMIT_BYTES = 56 * 1024 * 1024
MASK_LOGIT = -30000.0
SAFE_LOGIT_SPAN = 80.0

TOP_K = 2
MOE_TM = 1024
MOE_TF = 512
MOE_DMA_UNROLL = 8

LRU_CB = 128
LRU_TT = 128


def _largest_tile(n, target, mult=LANE):
    assert n % mult == 0, (n, mult)
    best = mult
    t = mult
    while t <= min(n, target):
        if n % t == 0:
            best = t
        t += mult
    return best


def _cparams(*sem):
    return pltpu.CompilerParams(dimension_semantics=sem, vmem_limit_bytes=VMEM_LIMIT_BYTES)


def _rmsnorm_rows(x, g):
    ms = jnp.mean(x * x, axis=-1, keepdims=True)
    return x * lax.rsqrt(ms + EPS) * g


def _in_proj_kernel(h_ref, g_ref, w_ref, z_ref, xn_ref):
    @pl.when(pl.program_id(1) == 0)
    def _():
        xn_ref[...] = _rmsnorm_rows(h_ref[...], g_ref[...]).astype(BF16)

    z_ref[...] = jnp.dot(xn_ref[...], w_ref[...], preferred_element_type=F32)


def _in_proj(h2, g, w):
    m = h2.shape[0]
    tm = _largest_tile(m, 1024)
    tn = _largest_tile(IN_WIDTH, 1792, 2 * LANE)
    return pl.pallas_call(
        _in_proj_kernel,
        grid=(m // tm, IN_WIDTH // tn),
        in_specs=[
            pl.BlockSpec((tm, D_MODEL), lambda i, j: (i, 0)),
            pl.BlockSpec((1, D_MODEL), lambda i, j: (0, 0)),
            pl.BlockSpec((D_MODEL, tn), lambda i, j: (0, j)),
        ],
        out_specs=pl.BlockSpec((tm, tn), lambda i, j: (i, j)),
        out_shape=jax.ShapeDtypeStruct((m, IN_WIDTH), F32),
        scratch_shapes=[pltpu.VMEM((tm, D_MODEL), BF16)],
        compiler_params=_cparams("parallel", "arbitrary"),
        name="in_proj",
    )(h2, g, w)


def _attn_kernel(sc_ref, q_ref, k_ref, v_ref, qg_ref, kg_ref, lam_ref, sg_ref, o_ref,
                 kb_ref, vb_ref, m_ref, l_ref, acc_ref, *, seq_len, tq, tk):
    lp = k_ref.shape[0]
    nk = lp // tk
    head = pl.program_id(1)
    qi = pl.program_id(2)
    lane = lax.broadcasted_iota(jnp.int32, (1, LANE), 1)
    lo = lane < HEAD_DIM

    def half_rmsnorm(x, g):
        sq = x * x
        s_lo = jnp.sum(jnp.where(lo, sq, 0.0), axis=-1, keepdims=True)
        s_hi = jnp.sum(jnp.where(lo, 0.0, sq), axis=-1, keepdims=True)
        ms = jnp.where(lo, s_lo, s_hi) * (1.0 / HEAD_DIM)
        return x * lax.rsqrt(ms + EPS) * g

    @pl.when(qi == 0)
    def _():
        def fill(c, carry):
            r = pl.ds(pl.multiple_of(c * tk, tk), tk)
            kb_ref[r, :] = half_rmsnorm(k_ref[r, :], kg_ref[...]).astype(BF16)
            vb_ref[r, :] = v_ref[r, :].astype(BF16)
            return carry
        lax.fori_loop(0, nk, fill, 0)

    qn = half_rmsnorm(q_ref[...], qg_ref[...]) * (1.0 / math.sqrt(HEAD_DIM))
    q_comp = (jnp.where(lo, qn, 0.0).astype(BF16), jnp.where(lo, 0.0, qn).astype(BF16))

    m_ref[...] = jnp.full(m_ref.shape, NEG_BIG, F32)
    l_ref[...] = jnp.zeros(l_ref.shape, F32)
    acc_ref[...] = jnp.zeros(acc_ref.shape, F32)

    slope = sc_ref[head]
    rel = (lax.broadcasted_iota(jnp.int32, (tq, tk), 0)
           - lax.broadcasted_iota(jnp.int32, (tq, tk), 1)).astype(F32)
    col = lax.broadcasted_iota(jnp.int32, (1, tk), 1)

    def chunk(kc, carry):
        k0 = pl.multiple_of(kc * tk, tk)
        kblk = kb_ref[pl.ds(k0, tk), :]
        vblk = vb_ref[pl.ds(k0, tk), :]
        off = (qi * tq - k0).astype(F32)
        bias = -slope * jnp.abs(rel + off) + jnp.where(col + k0 < seq_len, 0.0, NEG_BIG)
        for c in range(2):
            s = lax.dot_general(q_comp[c], kblk, (((1,), (1,)), ((), ())),
                                preferred_element_type=F32) + bias
            m_prev = m_ref[c]
            m_new = jnp.maximum(m_prev, jnp.max(s, axis=-1, keepdims=True))
            alpha = jnp.exp(m_prev - m_new)
            p = jnp.exp(s - m_new)
            l_ref[c] = alpha * l_ref[c] + jnp.sum(p, axis=-1, keepdims=True)
            acc_ref[c] = alpha * acc_ref[c] + jnp.dot(p.astype(BF16), vblk,
                                                      preferred_element_type=F32)
            m_ref[c] = m_new
        return carry

    lax.fori_loop(0, nk, chunk, 0)

    lam_init = sc_ref[N_HEADS]
    lp_ = lam_ref[...]
    lam = (jnp.exp(jnp.sum(lp_[0:1] * lp_[1:2], axis=-1, keepdims=True))
           - jnp.exp(jnp.sum(lp_[2:3] * lp_[3:4], axis=-1, keepdims=True)) + lam_init)
    o = acc_ref[0] / l_ref[0] - lam * (acc_ref[1] / l_ref[1])
    o = _rmsnorm_rows(o, sg_ref[...]) * (1.0 - lam_init)
    o_ref[...] = o.astype(BF16)


def _attn_bounded_kernel(sc_ref, q_ref, k_ref, v_ref, qg_ref, kg_ref, lam_ref, sg_ref, prev_ref,
                         o_ref, kq_ref, vt_ref, vtail_ref, lhs_ref, bias_ref, acc_ref,
                         *, seq_len, t, head0, reach):
    lp = k_ref.shape[0]
    s_main = lp - ATT_TAIL
    n_main = s_main // t
    del prev_ref
    slope = sc_ref[head0 + pl.program_id(1)]
    lam_init = sc_ref[N_HEADS]
    lane = lax.broadcasted_iota(jnp.int32, (1, LANE), 1)
    lo = lane < HEAD_DIM
    nt = (((1,), (1,)), ((), ()))

    def half_rmsnorm(x, g):
        sq = x * x
        s_lo = jnp.sum(jnp.where(lo, sq, 0.0), axis=-1, keepdims=True)
        s_hi = jnp.sum(jnp.where(lo, 0.0, sq), axis=-1, keepdims=True)
        ms = jnp.where(lo, s_lo, s_hi) * (1.0 / HEAD_DIM)
        return x * lax.rsqrt(ms + EPS) * g

    def split2(x):
        hi = x.astype(BF16).astype(F32)
        return hi, x - hi

    def pos_col(r0, n):
        return lax.broadcasted_iota(jnp.int32, (n, 1), 0) + r0

    lp_ = lam_ref[...]
    lam = (jnp.exp(jnp.sum(lp_[0:1] * lp_[1:2], axis=-1, keepdims=True))
           - jnp.exp(jnp.sum(lp_[2:3] * lp_[3:4], axis=-1, keepdims=True)) + lam_init)

    ones_row = jnp.where(lax.broadcasted_iota(jnp.int32, (VT_ROWS - V_DIM, 1), 0) == 0, 1.0, 0.0)

    def fill(r0, n, vt_dst):
        rows = pos_col(r0, n)
        valid = rows < seq_len
        kn = jnp.where(valid, half_rmsnorm(k_ref[pl.ds(r0, n), :], kg_ref[...]), 0.0)
        sj_hi, sj_lo = split2(slope * rows.astype(F32))
        aug = jnp.where(lane == 0, -1.0,
              jnp.where((lane == 1) | (lane == 2), 1.0,
              jnp.where(lane == 3, sj_hi,
              jnp.where(lane == 4, sj_lo,
              jnp.where(lane == 5, jnp.where(valid, 0.0, MASK_LOGIT), 0.0)))))
        kq_ref[pl.ds(r0, n), 0:LANE] = kn.astype(BF16)
        kq_ref[pl.ds(r0, n), LANE:2 * LANE] = aug.astype(BF16)
        vt = jnp.transpose(jnp.where(valid, v_ref[pl.ds(r0, n), :], 0.0))
        vt_dst[...] = jnp.concatenate(
            [vt, jnp.broadcast_to(ones_row, (VT_ROWS - V_DIM, n))], axis=0).astype(BF16)

    def fill_chunk(c, carry):
        fill(pl.multiple_of(c * t, t), t, vt_ref.at[c])
        return carry
    lax.fori_loop(0, n_main, fill_chunk, 0)
    fill(s_main, ATT_TAIL, vtail_ref)

    def comp_max(g, c):
        return jnp.max(jnp.where(lo == (c == 0), jnp.abs(g), 0.0), axis=-1, keepdims=True)
    bounds = [math.sqrt(HEAD_DIM) * comp_max(qg_ref[...], c) * comp_max(kg_ref[...], c)
              for c in range(2)]

    rel = (lax.broadcasted_iota(jnp.int32, (t, t), 0)
           - lax.broadcasted_iota(jnp.int32, (t, t), 1)).astype(F32)
    bias_ref[0] = jnp.zeros((t, t), F32)
    bias_ref[1] = -slope * jnp.abs(rel)

    def build_lhs(r0, n):
        rows = pos_col(r0, n)
        qn = half_rmsnorm(q_ref[pl.ds(r0, n), :], qg_ref[...]) * (1.0 / math.sqrt(HEAD_DIM))
        si_hi, si_lo = split2(slope * rows.astype(F32))
        side = jnp.where(lane == 1, si_hi,
               jnp.where(lane == 2, si_lo,
               jnp.where((lane == 3) | (lane == 4), -1.0, 0.0)))
        for c in range(2):
            qc = jnp.where(lo, qn, 0.0) if c == 0 else jnp.where(lo, 0.0, qn)
            base = jnp.broadcast_to(
                jnp.where(lane == 0, bounds[c], jnp.where(lane == 5, 1.0, 0.0)), (n, LANE))
            qcb = qc.astype(BF16)
            for x, aug in enumerate((base - side, base, base + side)):
                lhs_ref[x, c * n:(c + 1) * n, 0:LANE] = qcb
                lhs_ref[x, c * n:(c + 1) * n, LANE:2 * LANE] = aug.astype(BF16)

    def process(x, n, k0, nk, vt, bias):
        s = lax.dot_general(kq_ref[pl.ds(k0, nk), :], lhs_ref[x, 0:2 * n, :], nt,
                            preferred_element_type=F32)
        if bias is not None:
            s = s + jnp.concatenate([bias, bias], axis=1)
        p = jnp.exp(s).astype(BF16)
        acc_ref[:, 0:2 * n] += jnp.dot(vt, p, preferred_element_type=F32)

    def finalize(r0, n):
        a0 = acc_ref[:, 0:n]
        a1 = acc_ref[:, n:2 * n]
        o = (a0[0:V_DIM] * (1.0 / a0[V_DIM:V_DIM + 1])
             - (lam * (1.0 / a1[V_DIM:V_DIM + 1])) * a1[0:V_DIM])
        ms = jnp.mean(o * o, axis=0, keepdims=True)
        o = o * lax.rsqrt(ms + EPS) * sg_ref[...] * (1.0 - lam_init)
        pos = lax.broadcasted_iota(jnp.int32, (1, n), 1) + r0
        o = jnp.where(pos < seq_len, o, 0.0)
        o_ref[pl.ds(r0, n), :] = jnp.transpose(o).astype(BF16)

    window = min(2 * reach + 1, n_main)

    def full_chunks(qi, n, kc0, count):
        def body(w, carry):
            kc = kc0 + w
            x = jnp.where(kc < qi, 0, jnp.where(kc == qi, 1, 2))
            d = jnp.where(kc == qi, 1, 0)
            process(x, n, pl.multiple_of(kc * t, t), t, vt_ref[kc], bias_ref[d, :, 0:n])
            return carry
        lax.fori_loop(0, count, body, 0, unroll=min(count, ATT_MAX_UNROLL))

    def q_main(qi, carry):
        r0 = pl.multiple_of(qi * t, t)
        build_lhs(r0, t)
        acc_ref[...] = jnp.zeros(acc_ref.shape, F32)
        full_chunks(qi, t, jnp.clip(qi - reach, 0, n_main - window), window)

        @pl.when(n_main - qi <= reach)
        def _():
            process(2, t, s_main, ATT_TAIL, vtail_ref[...], None)
        finalize(r0, t)
        return carry
    lax.fori_loop(0, n_main, q_main, 0)

    build_lhs(s_main, ATT_TAIL)
    acc_ref[...] = jnp.zeros(acc_ref.shape, F32)
    tail_count = min(reach, n_main)
    full_chunks(n_main, ATT_TAIL, n_main - tail_count, tail_count)
    process(1, ATT_TAIL, s_main, ATT_TAIL, vtail_ref[...], bias_ref[1, 0:ATT_TAIL, 0:ATT_TAIL])
    finalize(s_main, ATT_TAIL)


def _chunk_reach(head, t):
    slope = 2.0 ** (-8.0 * (head + 1) / N_HEADS)
    return max(1, math.ceil((EXP_ZERO_LOGIT / slope - 1.0) / t))


def _attention_bounded(z3, scal, qg, kg, lamp, sg, seq_len):
    b, lp, _ = z3.shape
    t = _largest_tile(lp - ATT_TAIL, ATT_TILE)
    n_main = (lp - ATT_TAIL) // t
    groups = []
    for head in range(N_HEADS):
        reach = min(_chunk_reach(head, t), n_main)
        if groups and min(2 * groups[-1][2] + 1, n_main) == min(2 * reach + 1, n_main):
            groups[-1][1] += 1
            groups[-1][2] = max(groups[-1][2], reach)
        else:
            groups.append([head, 1, reach])
    small = lambda shape: pl.BlockSpec(shape, lambda bi, hi: (0, 0))
    att = jnp.zeros((b, lp, N_HEADS * V_DIM), BF16)
    for head0, count, reach in groups:
        kern = functools.partial(_attn_bounded_kernel, seq_len=seq_len, t=t, head0=head0,
                                 reach=reach)
        col = lambda off, head0=head0: pl.BlockSpec(
            (None, lp, V_DIM), lambda bi, hi: (bi, 0, off + head0 + hi))
        att = pl.pallas_call(
            kern,
            grid=(b, count),
            in_specs=[
                pl.BlockSpec(memory_space=pltpu.SMEM),
                col(0), col(N_HEADS), col(2 * N_HEADS),
                small((1, V_DIM)), small((1, V_DIM)), small((4, HEAD_DIM)), small((V_DIM, 1)),
                pl.BlockSpec(memory_space=pl.ANY),
            ],
            out_specs=col(0),
            out_shape=jax.ShapeDtypeStruct(att.shape, att.dtype),
            input_output_aliases={8: 0},
            scratch_shapes=[
                pltpu.VMEM((lp, 2 * LANE), BF16),
                pltpu.VMEM((n_main, VT_ROWS, t), BF16), pltpu.VMEM((VT_ROWS, ATT_TAIL), BF16),
                pltpu.VMEM((3, 2 * t, 2 * LANE), BF16),
                pltpu.VMEM((2, t, t), F32),
                pltpu.VMEM((VT_ROWS, 2 * t), F32),
            ],
            compiler_params=pltpu.CompilerParams(dimension_semantics=("parallel", "parallel"),
                                                 vmem_limit_bytes=ATT_VMEM_LIMIT_BYTES),
            name="diff_attention_bounded",
        )(scal, z3, z3, z3, qg, kg, lamp, sg.reshape(V_DIM, 1), att)
    return att


def _attention(z3, scal, qg, kg, lamp, sg, seq_len):
    span = 2.0 * HEAD_DIM / math.sqrt(HEAD_DIM) * jnp.max(jnp.abs(qg)) * jnp.max(jnp.abs(kg))
    args = (z3, scal, qg, kg, lamp, sg)
    return lax.cond(span <= SAFE_LOGIT_SPAN,
                    lambda *a: _attention_bounded(*a, seq_len),
                    lambda *a: _attention_online(*a, seq_len), *args)


def _attention_online(z3, scal, qg, kg, lamp, sg, seq_len):
    b, lp, _ = z3.shape
    tq = _largest_tile(lp, 768)
    tk = tq
    kern = functools.partial(_attn_kernel, seq_len=seq_len, tq=tq, tk=tk)
    small = lambda shape: pl.BlockSpec(shape, lambda bi, hi, qi: (0, 0))
    return pl.pallas_call(
        kern,
        grid=(b, N_HEADS, lp // tq),
        in_specs=[
            pl.BlockSpec(memory_space=pltpu.SMEM),
            pl.BlockSpec((None, tq, V_DIM), lambda bi, hi, qi: (bi, qi, hi)),
            pl.BlockSpec((None, lp, V_DIM), lambda bi, hi, qi: (bi, 0, N_HEADS + hi)),
            pl.BlockSpec((None, lp, V_DIM), lambda bi, hi, qi: (bi, 0, 2 * N_HEADS + hi)),
            small((1, V_DIM)), small((1, V_DIM)), small((4, HEAD_DIM)), small((1, V_DIM)),
        ],
        out_specs=pl.BlockSpec((None, tq, V_DIM), lambda bi, hi, qi: (bi, qi, hi)),
        out_shape=jax.ShapeDtypeStruct((b, lp, N_HEADS * V_DIM), BF16),
        scratch_shapes=[
            pltpu.VMEM((lp, V_DIM), BF16), pltpu.VMEM((lp, V_DIM), BF16),
            pltpu.VMEM((2, tq, 1), F32), pltpu.VMEM((2, tq, 1), F32),
            pltpu.VMEM((2, tq, V_DIM), F32),
        ],
        compiler_params=_cparams("parallel", "parallel", "arbitrary"),
        name="diff_attention",
    )(scal, z3, z3, z3, qg, kg, lamp, sg)


def _lru_kernel(xl_ref, gl_ref, cw_ref, cbias_ref, w_ref, ba_ref, bx_ref, lam_ref, o_ref,
                xp_ref, hf_ref, hb_ref, *, seq_len, tt):
    lp, cb = xl_ref.shape
    n = lp // tt
    pad = 8
    rows_t = lax.broadcasted_iota(jnp.int32, (tt, 1), 0)

    xp_ref[0:pad, :] = jnp.zeros((pad, cb), F32)
    xp_ref[lp + pad:lp + 2 * pad, :] = jnp.zeros((pad, cb), F32)

    def copy_in(c, carry):
        r0 = pl.multiple_of(c * tt, tt)
        x = xl_ref[pl.ds(r0, tt), :]
        xp_ref[pl.ds(r0 + pad, tt), :] = jnp.where(rows_t + r0 < seq_len, x, 0.0)
        return carry
    lax.fori_loop(0, n, copy_in, 0)

    cw = cw_ref[...]

    def gates(c, d):
        r0 = pl.multiple_of(c * tt, tt)
        win = xp_ref[pl.ds(r0, tt + 2 * pad), :]
        wn = tt + 2 * pad
        xc = (pltpu.roll(win, 2, 0)[pad:pad + tt] * cw[0:1]
              + pltpu.roll(win, 1, 0)[pad:pad + tt] * cw[1:2]
              + win[pad:pad + tt] * cw[2:3]
              + pltpu.roll(win, wn - 1, 0)[pad:pad + tt] * cw[3:4]
              + cbias_ref[...])
        pre = jnp.dot(xc.astype(BF16), w_ref[:, d * 2 * cb:(d + 1) * 2 * cb],
                      preferred_element_type=F32)
        r = jax.nn.sigmoid(pre[:, :cb] + ba_ref[d:d + 1, :])
        i = jax.nn.sigmoid(pre[:, cb:] + bx_ref[d:d + 1, :])
        nl = -lam_ref[d:d + 1, :]
        softplus = jnp.maximum(nl, 0.0) + jnp.log1p(jnp.exp(-jnp.abs(nl)))
        log_a = (-LRU_C * r) * softplus
        a = jnp.exp(log_a)
        th = jnp.tanh(log_a)
        m2 = -2.0 * th / (1.0 - th)
        mult = jnp.where(m2 > 0.0, m2 * lax.rsqrt(m2), 0.0)
        u = mult * (i * xc)
        u = jnp.where(rows_t + r0 < seq_len, u, 0.0)
        return a, u

    ng = tt // SUBLANE
    sub = lax.broadcasted_iota(jnp.int32, (1, SUBLANE, 1), 1)

    def scan_chunk(a, u, carry, reverse):
        a3 = a.reshape(ng, SUBLANE, cb)
        u3 = u.reshape(ng, SUBLANE, cb)
        d = 1
        while d < SUBLANE:
            if reverse:
                shift, keep = SUBLANE - d, sub < SUBLANE - d
            else:
                shift, keep = d, sub >= d
            u3 = u3 + a3 * jnp.where(keep, pltpu.roll(u3, shift, 1), 0.0)
            a3 = a3 * jnp.where(keep, pltpu.roll(a3, shift, 1), 1.0)
            d *= 2
        hs = [None] * ng
        for g in (range(ng - 1, -1, -1) if reverse else range(ng)):
            hg = u3[g] + a3[g] * carry
            hs[g] = hg
            carry = hg[0:1, :] if reverse else hg[SUBLANE - 1:SUBLANE, :]
        return jnp.concatenate(hs, axis=0), carry

    def both(j, carry):
        cf = j
        h, state_f = scan_chunk(*gates(cf, 0), carry[0], False)
        hf_ref[pl.ds(pl.multiple_of(cf * tt, tt), tt), :] = h
        cr = n - 1 - j
        h, state_b = scan_chunk(*gates(cr, 1), carry[1], True)
        hb_ref[pl.ds(pl.multiple_of(cr * tt, tt), tt), :] = h
        return state_f, state_b
    zero_state = jnp.zeros((1, cb), F32)
    lax.fori_loop(0, n, both, (zero_state, zero_state))

    def combine(c, carry):
        r = pl.ds(pl.multiple_of(c * tt, tt), tt)
        o_ref[r, :] = ((hf_ref[r, :] + hb_ref[r, :]) * jax.nn.gelu(gl_ref[r, :])).astype(BF16)
        return carry
    lax.fori_loop(0, n, combine, 0)


def _lru(z3, cw, cbias, wcat, ba, bx, lam, seq_len):
    b, lp, _ = z3.shape
    cb = LRU_CB
    nc = D_MODEL // cb
    kern = functools.partial(_lru_kernel, seq_len=seq_len, tt=LRU_TT)
    chan = lambda rows: pl.BlockSpec((rows, cb), lambda bi, ci: (0, ci))
    return pl.pallas_call(
        kern,
        grid=(b, nc),
        in_specs=[
            pl.BlockSpec((None, lp, cb), lambda bi, ci: (bi, 0, 3 * nc + ci)),
            pl.BlockSpec((None, lp, cb), lambda bi, ci: (bi, 0, 4 * nc + ci)),
            chan(4), chan(1),
            pl.BlockSpec((None, cb, 4 * cb), lambda bi, ci: (ci, 0, 0)),
            chan(2), chan(2), chan(2),
        ],
        out_specs=pl.BlockSpec((None, lp, cb), lambda bi, ci: (bi, 0, ci)),
        out_shape=jax.ShapeDtypeStruct((b, lp, D_MODEL), BF16),
        scratch_shapes=[pltpu.VMEM((lp + 16, cb), F32), pltpu.VMEM((lp, cb), F32),
                        pltpu.VMEM((lp, cb), F32)],
        compiler_params=_cparams("parallel", "parallel"),
        name="rglru",
    )(z3, z3, cw, cbias, wcat, ba, bx, lam)


def _merge_kernel(att_ref, yl_ref, ga_ref, gr_ref, h_ref, wa_ref, wl_ref, wo_ref,
                  bga_ref, bgr_ref, o_ref):
    ya = jnp.dot(att_ref[...], wa_ref[...], preferred_element_type=F32)
    yr = jnp.dot(yl_ref[...], wl_ref[...], preferred_element_type=F32)
    merged = (jax.nn.sigmoid(ga_ref[...] + bga_ref[...]) * ya
              + jax.nn.sigmoid(gr_ref[...] + bgr_ref[...]) * yr)
    o_ref[...] = h_ref[...] + jnp.dot(merged.astype(BF16), wo_ref[...],
                                      preferred_element_type=F32)


def _merge(att2, yl2, z2, h2, wa, wl, wo, bga, bgr):
    m = h2.shape[0]
    tm = _largest_tile(m, 512)
    row = lambda col: pl.BlockSpec((tm, D_MODEL), lambda i: (i, col))
    full = lambda shape: pl.BlockSpec(shape, lambda i: (0, 0))
    return pl.pallas_call(
        _merge_kernel,
        grid=(m // tm,),
        in_specs=[row(0), row(0), row(5), row(6), row(0),
                  full((D_MODEL, D_MODEL)), full((D_MODEL, D_MODEL)), full((D_MODEL, D_MODEL)),
                  full((1, D_MODEL)), full((1, D_MODEL))],
        out_specs=row(0),
        out_shape=jax.ShapeDtypeStruct((m, D_MODEL), F32),
        compiler_params=_cparams("parallel"),
        name="merge_out_proj",
    )(att2, yl2, z2, z2, h2, wa, wl, wo, bga, bgr)


def _ffn_kernel(*refs, chunks_per_expert):
    if chunks_per_expert:
        h_ref, g_ref, wr_ref, wg_ref, wu_ref, wd_ref, o_ref, hn_ref, acc_ref, gate_ref = refs
    else:
        h_ref, g_ref, wg_ref, wu_ref, wd_ref, o_ref, hn_ref, acc_ref = refs
    j = pl.program_id(1)

    @pl.when(j == 0)
    def _():
        h = h_ref[...]
        hn = _rmsnorm_rows(h, g_ref[...])
        hn_ref[...] = hn.astype(BF16)
        acc_ref[...] = h
        if chunks_per_expert:
            logits = jnp.dot(hn, wr_ref[...], preferred_element_type=F32,
                             precision=lax.Precision.HIGHEST)
            lane = lax.broadcasted_iota(jnp.int32, logits.shape, 1).astype(F32)
            lg = jnp.where(lane < N_EXPERTS, logits, NEG_BIG)
            m1 = jnp.max(lg, axis=-1, keepdims=True)
            i1 = jnp.min(jnp.where(lg == m1, lane, float(LANE)), axis=-1, keepdims=True)
            lg2 = jnp.where(lane == i1, NEG_BIG, lg)
            m2 = jnp.max(lg2, axis=-1, keepdims=True)
            i2 = jnp.min(jnp.where(lg2 == m2, lane, float(LANE)), axis=-1, keepdims=True)
            e2 = jnp.exp(m2 - m1)
            den = 1.0 + e2
            gate_ref[...] = jnp.where(lane == i1, 1.0 / den,
                                      jnp.where(lane == i2, e2 / den, 0.0))

    hn = hn_ref[...]
    gg = jnp.dot(hn, wg_ref[...], preferred_element_type=F32)
    uu = jnp.dot(hn, wu_ref[...], preferred_element_type=F32)
    act = (jax.nn.silu(gg) * uu).astype(BF16)
    y = jnp.dot(act, wd_ref[...], preferred_element_type=F32)
    if chunks_per_expert:
        e = (j // chunks_per_expert).astype(F32)
        gates = gate_ref[...]
        lane = lax.broadcasted_iota(jnp.int32, gates.shape, 1).astype(F32)
        y = y * jnp.sum(jnp.where(lane == e, gates, 0.0), axis=-1, keepdims=True)
    acc_ref[...] += y

    @pl.when(j == pl.num_programs(1) - 1)
    def _():
        o_ref[...] = acc_ref[...]


def _ffn(h2, g, wg, wu, wd, router=None, d_ff_expert=None):
    m = h2.shape[0]
    f = wg.shape[1]
    tm = _largest_tile(m, 1024)
    tf = 512
    in_specs = [pl.BlockSpec((tm, D_MODEL), lambda i, j: (i, 0)),
                pl.BlockSpec((1, D_MODEL), lambda i, j: (0, 0))]
    args = [h2, g]
    scratch = [pltpu.VMEM((tm, D_MODEL), BF16), pltpu.VMEM((tm, D_MODEL), F32)]
    cpe = 0
    if router is not None:
        assert d_ff_expert % tf == 0
        cpe = d_ff_expert // tf
        in_specs.append(pl.BlockSpec((D_MODEL, LANE), lambda i, j: (0, 0)))
        args.append(router)
        scratch.append(pltpu.VMEM((tm, LANE), F32))
    in_specs += [pl.BlockSpec((D_MODEL, tf), lambda i, j: (0, j)),
                 pl.BlockSpec((D_MODEL, tf), lambda i, j: (0, j)),
                 pl.BlockSpec((tf, D_MODEL), lambda i, j: (j, 0))]
    args += [wg, wu, wd]
    return pl.pallas_call(
        functools.partial(_ffn_kernel, chunks_per_expert=cpe),
        grid=(m // tm, f // tf),
        in_specs=in_specs,
        out_specs=pl.BlockSpec((tm, D_MODEL), lambda i, j: (i, 0)),
        out_shape=jax.ShapeDtypeStruct((m, D_MODEL), F32),
        scratch_shapes=scratch,
        compiler_params=_cparams("parallel", "arbitrary"),
        name="moe_ffn" if cpe else "dense_ffn",
    )(*args)


def _moe_router_kernel(h_ref, g_ref, wr_ref, r_ref, cnt_ref, tri_ref, base_ref):
    tm = h_ref.shape[0]

    @pl.when(pl.program_id(0) == 0)
    def _():
        row = lax.broadcasted_iota(jnp.int32, (tm, tm), 0)
        col = lax.broadcasted_iota(jnp.int32, (tm, tm), 1)
        tri_ref[...] = jnp.where(col < row, 1.0, 0.0).astype(BF16)
        base_ref[...] = jnp.zeros(base_ref.shape, F32)

    hn = _rmsnorm_rows(h_ref[...], g_ref[...])
    logits = jnp.dot(hn, wr_ref[...], preferred_element_type=F32, precision=lax.Precision.HIGHEST)
    lane = lax.broadcasted_iota(jnp.int32, logits.shape, 1).astype(F32)
    lg = jnp.where(lane < N_EXPERTS, logits, NEG_BIG)
    m1 = jnp.max(lg, axis=-1, keepdims=True)
    i1 = jnp.min(jnp.where(lg == m1, lane, float(LANE)), axis=-1, keepdims=True)
    lg2 = jnp.where(lane == i1, NEG_BIG, lg)
    m2 = jnp.max(lg2, axis=-1, keepdims=True)
    i2 = jnp.min(jnp.where(lg2 == m2, lane, float(LANE)), axis=-1, keepdims=True)
    e2 = jnp.exp(m2 - m1)
    den = 1.0 + e2
    oh1 = lane == i1
    oh2 = lane == i2
    assigned = jnp.where(oh1 | oh2, 1.0, 0.0)
    before = jnp.dot(tri_ref[...], assigned.astype(BF16), preferred_element_type=F32) + base_ref[...]
    rank1 = jnp.sum(jnp.where(oh1, before, 0.0), axis=-1, keepdims=True)
    rank2 = jnp.sum(jnp.where(oh2, before, 0.0), axis=-1, keepdims=True)
    r_ref[...] = jnp.where(lane == 0, i1, jnp.where(lane == 1, i2,
                 jnp.where(lane == 2, 1.0 / den, jnp.where(lane == 3, e2 / den,
                 jnp.where(lane == 4, rank1, jnp.where(lane == 5, rank2, 0.0))))))
    base_ref[...] += jnp.sum(assigned, axis=0, keepdims=True)
    cnt_ref[...] = base_ref[...]


def _moe_router(h2, g, router):
    m = h2.shape[0]
    tm = _largest_tile(m, 1024)
    return pl.pallas_call(
        _moe_router_kernel,
        grid=(m // tm,),
        in_specs=[pl.BlockSpec((tm, D_MODEL), lambda i: (i, 0)),
                  pl.BlockSpec((1, D_MODEL), lambda i: (0, 0)),
                  pl.BlockSpec((D_MODEL, LANE), lambda i: (0, 0))],
        out_specs=[pl.BlockSpec((tm, LANE), lambda i: (i, 0)),
                   pl.BlockSpec((1, LANE), lambda i: (0, 0))],
        out_shape=[jax.ShapeDtypeStruct((m, LANE), F32), jax.ShapeDtypeStruct((1, LANE), F32)],
        scratch_shapes=[pltpu.VMEM((tm, tm), BF16), pltpu.VMEM((1, LANE), F32)],
        compiler_params=_cparams("arbitrary"),
        name="moe_router",
    )(h2, g, router)


def _slab_copy(src_ref, src_tok, dst_ref, dst_tok, sem):
    return pltpu.make_async_copy(
        src_ref.at[pl.ds(pl.multiple_of(src_tok * SUBLANE, SUBLANE), SUBLANE), :],
        dst_ref.at[pl.ds(pl.multiple_of(dst_tok * SUBLANE, SUBLANE), SUBLANE), :], sem)


def _rows_to_slabs(x, slab_ref):
    tm = x.shape[0]
    for s in range(D_MODEL // LANE):
        slab_ref[pl.ds(s, tm, stride=SUBLANE), :] = x[:, s * LANE:(s + 1) * LANE]


def _slabs_lane_block(slab_ref, s, tm):
    return slab_ref[pl.ds(s, tm, stride=SUBLANE), :]


def _moe_scatter_kernel(pos_ref, h_ref, g_ref, xs_in_ref, xs_ref, slab_ref, sem):
    del xs_in_ref
    tm = h_ref.shape[0]
    _rows_to_slabs(_rmsnorm_rows(h_ref[...], g_ref[...]), slab_ref)

    def body(r, carry):
        for s in range(TOP_K):
            _slab_copy(slab_ref, r, xs_ref, pos_ref[s, r], sem).start(priority=s)
        return carry
    lax.fori_loop(0, tm, body, 0, unroll=MOE_DMA_UNROLL)
    for s in range(TOP_K):
        pltpu.make_async_copy(slab_ref, xs_ref.at[pl.ds(0, tm * SUBLANE), :], sem).wait()


def _moe_scatter(pos, h2, g, n_rows):
    m = h2.shape[0]
    tm = _largest_tile(m, MOE_TM)
    xs0 = jnp.zeros((n_rows * SUBLANE, LANE), F32)
    return pl.pallas_call(
        _moe_scatter_kernel,
        grid=(m // tm,),
        in_specs=[pl.BlockSpec((TOP_K, tm), lambda i: (0, i), memory_space=pltpu.SMEM),
                  pl.BlockSpec((tm, D_MODEL), lambda i: (i, 0)),
                  pl.BlockSpec((1, D_MODEL), lambda i: (0, 0)),
                  pl.BlockSpec(memory_space=pl.ANY)],
        out_specs=pl.BlockSpec(memory_space=pl.ANY),
        out_shape=jax.ShapeDtypeStruct(xs0.shape, xs0.dtype),
        input_output_aliases={3: 0},
        scratch_shapes=[pltpu.VMEM((tm * SUBLANE, LANE), F32), pltpu.SemaphoreType.DMA(())],
        compiler_params=_cparams("arbitrary"),
        name="moe_scatter",
    )(pos, h2, g, xs0)


def _moe_expert_kernel(te_ref, x_ref, wg_ref, wu_ref, wd_ref, y_ref, xb_ref, acc_ref):
    del te_ref
    j = pl.program_id(1)
    tm = xb_ref.shape[0]

    @pl.when(j == 0)
    def _():
        for s in range(D_MODEL // LANE):
            xb_ref[:, s * LANE:(s + 1) * LANE] = _slabs_lane_block(x_ref, s, tm).astype(BF16)
        acc_ref[...] = jnp.zeros(acc_ref.shape, F32)

    xb = xb_ref[...]
    gg = jnp.dot(xb, wg_ref[...], preferred_element_type=F32)
    uu = jnp.dot(xb, wu_ref[...], preferred_element_type=F32)
    act = (jax.nn.silu(gg) * uu).astype(BF16)
    acc_ref[...] += jnp.dot(act, wd_ref[...], preferred_element_type=F32)

    @pl.when(j == pl.num_programs(1) - 1)
    def _():
        _rows_to_slabs(acc_ref[...], y_ref)


def _moe_experts(tile_expert, xs, wg, wu, wd):
    n_rows = xs.shape[0] // SUBLANE
    fe = wg.shape[2]
    tm, tf = MOE_TM, MOE_TF
    slab_tile = pl.BlockSpec((tm * SUBLANE, LANE), lambda i, j, te: (i, 0))
    grid_spec = pltpu.PrefetchScalarGridSpec(
        num_scalar_prefetch=1,
        grid=(n_rows // tm, fe // tf),
        in_specs=[slab_tile,
                  pl.BlockSpec((None, D_MODEL, tf), lambda i, j, te: (te[i], 0, j)),
                  pl.BlockSpec((None, D_MODEL, tf), lambda i, j, te: (te[i], 0, j)),
                  pl.BlockSpec((None, tf, D_MODEL), lambda i, j, te: (te[i], j, 0))],
        out_specs=slab_tile,
        scratch_shapes=[pltpu.VMEM((tm, D_MODEL), BF16), pltpu.VMEM((tm, D_MODEL), F32)])
    return pl.pallas_call(
        _moe_expert_kernel,
        grid_spec=grid_spec,
        out_shape=jax.ShapeDtypeStruct(xs.shape, F32),
        compiler_params=_cparams("parallel", "arbitrary"),
        name="moe_experts",
    )(tile_expert, xs, wg, wu, wd)


def _moe_combine_kernel(pos_ref, r_ref, h_ref, ys_ref, o_ref, y0_ref, y1_ref, sem):
    tm = h_ref.shape[0]
    bufs = (y0_ref, y1_ref)

    def body(r, carry):
        for s in range(TOP_K):
            _slab_copy(ys_ref, pos_ref[s, r], bufs[s], r, sem).start(priority=s)
        return carry
    lax.fori_loop(0, tm, body, 0, unroll=MOE_DMA_UNROLL)
    for s in range(TOP_K):
        pltpu.make_async_copy(ys_ref.at[pl.ds(0, tm * SUBLANE), :], bufs[s], sem).wait()
    r = r_ref[...]
    for s in range(D_MODEL // LANE):
        o_ref[:, s * LANE:(s + 1) * LANE] = h_ref[:, s * LANE:(s + 1) * LANE] + (
            r[:, 2:3] * _slabs_lane_block(y0_ref, s, tm) + r[:, 3:4] * _slabs_lane_block(y1_ref, s, tm))


def _moe_combine(pos, route, h2, ys):
    m = h2.shape[0]
    tm = _largest_tile(m, MOE_TM)
    return pl.pallas_call(
        _moe_combine_kernel,
        grid=(m // tm,),
        in_specs=[pl.BlockSpec((TOP_K, tm), lambda i: (0, i), memory_space=pltpu.SMEM),
                  pl.BlockSpec((tm, LANE), lambda i: (i, 0)),
                  pl.BlockSpec((tm, D_MODEL), lambda i: (i, 0)),
                  pl.BlockSpec(memory_space=pl.ANY)],
        out_specs=pl.BlockSpec((tm, D_MODEL), lambda i: (i, 0)),
        out_shape=jax.ShapeDtypeStruct((m, D_MODEL), F32),
        scratch_shapes=[pltpu.VMEM((tm * SUBLANE, LANE), F32), pltpu.VMEM((tm * SUBLANE, LANE), F32),
                        pltpu.SemaphoreType.DMA(())],
        compiler_params=_cparams("arbitrary"),
        name="moe_combine",
    )(pos, route, h2, ys)


def _moe(h2, g, router, wg, wu, wd):
    m = h2.shape[0]
    route, counts = _moe_router(h2, g, router)
    cnt = counts[0, :N_EXPERTS].astype(jnp.int32)
    padded = (cnt + MOE_TM - 1) // MOE_TM * MOE_TM
    ends = jnp.cumsum(padded)
    starts = ends - padded
    expert = route[:, 0:TOP_K].astype(jnp.int32)
    pos = (starts[expert] + route[:, 4:4 + TOP_K].astype(jnp.int32)).T
    n_rows = TOP_K * m + N_EXPERTS * MOE_TM
    tile_start = jnp.arange(n_rows // MOE_TM, dtype=jnp.int32) * MOE_TM
    tile_expert = jnp.minimum(jnp.searchsorted(ends, tile_start, side='right'),
                              N_EXPERTS - 1).astype(jnp.int32)
    xs = _moe_scatter(pos, h2, g, n_rows)
    ys = _moe_experts(tile_expert, xs, wg, wu, wd)
    return _moe_combine(pos, route, h2, ys)


def _blockdiag_groups(w, per):
    g = LRU_BLOCKS // per
    w = w.reshape(g, per, LRU_BLOCK_W, LRU_BLOCK_W)
    eye = jnp.eye(per, dtype=w.dtype)
    out = jnp.einsum('gpio,pq->gpiqo', w, eye)
    return out.reshape(g, per * LRU_BLOCK_W, per * LRU_BLOCK_W)


def _prepare(p):
    depth = p['w_in'].shape[0]
    per = LRU_CB // LRU_BLOCK_W
    layers = []
    slopes = jnp.exp2(-8.0 * jnp.arange(1, N_HEADS + 1, dtype=F32) / N_HEADS)
    for l in range(depth):
        lam_init = 0.8 - 0.6 * math.exp(-0.3 * l)
        wcat = jnp.concatenate(
            [_blockdiag_groups(p['lru_wa'][l, 0], per), _blockdiag_groups(p['lru_wx'][l, 0], per),
             _blockdiag_groups(p['lru_wa'][l, 1], per), _blockdiag_groups(p['lru_wx'][l, 1], per)],
            axis=-1).astype(BF16)
        lay = dict(
            norm_mix=p['norm_mix'][l][None], norm_ffn=p['norm_ffn'][l][None],
            w_in=p['w_in'][l].astype(BF16),
            scal=jnp.concatenate([slopes, jnp.full((N_HEADS,), lam_init, F32)]),
            qg=jnp.tile(p['q_norm'][l], 2)[None], kg=jnp.tile(p['k_norm'][l], 2)[None],
            lamp=jnp.stack([p['lambda_q1'][l], p['lambda_k1'][l],
                            p['lambda_q2'][l], p['lambda_k2'][l]]),
            sg=p['attn_subln'][l][None],
            conv_w=p['conv_w'][l], conv_b=p['conv_b'][l][None], wcat=wcat,
            ba=p['lru_ba'][l], bx=p['lru_bx'][l], lam=p['lru_lambda'][l],
            wa=p['w_attn_branch'][l].astype(BF16), wl=p['w_lru_branch'][l].astype(BF16),
            wo=p['w_out'][l].astype(BF16),
            bga=p['b_gate'][l, :D_MODEL][None], bgr=p['b_gate'][l, D_MODEL:][None],
        )
        j = l // 2
        if l % 2 == 0:
            lay.update(wg=p['ffn_w_gate'][j].astype(BF16), wu=p['ffn_w_up'][j].astype(BF16),
                       wd=p['ffn_w_down'][j].astype(BF16))
        else:
            lay.update(
                wg=p['moe_w_gate'][j].astype(BF16), wu=p['moe_w_up'][j].astype(BF16),
                wd=p['moe_w_down'][j].astype(BF16),
                router=jnp.pad(p['moe_router'][j], ((0, 0), (0, LANE - N_EXPERTS))))
        layers.append(lay)
    return layers


def _trunk(x, meta, layers):
    b, s, _ = x.shape
    seq_len = s + N_META
    lp = -(-seq_len // LANE) * LANE
    h = jnp.concatenate(
        [jnp.broadcast_to(meta[None].astype(x.dtype), (b, N_META, D_MODEL)), x,
         jnp.zeros((b, lp - seq_len, D_MODEL), x.dtype)], axis=1)
    h2 = h.reshape(b * lp, D_MODEL)
    for lay in layers:
        z2 = _in_proj(h2, lay['norm_mix'], lay['w_in'])
        z3 = z2.reshape(b, lp, IN_WIDTH)
        att = _attention(z3, lay['scal'], lay['qg'], lay['kg'], lay['lamp'], lay['sg'], seq_len)
        yl = _lru(z3, lay['conv_w'], lay['conv_b'], lay['wcat'], lay['ba'], lay['bx'],
                  lay['lam'], seq_len)
        h2 = _merge(att.reshape(b * lp, D_MODEL), yl.reshape(b * lp, D_MODEL), z2, h2,
                    lay['wa'], lay['wl'], lay['wo'], lay['bga'], lay['bgr'])
        if 'router' in lay:
            h2 = _moe(h2, lay['norm_ffn'], lay['router'], lay['wg'], lay['wu'], lay['wd'])
        else:
            h2 = _ffn(h2, lay['norm_ffn'], lay['wg'], lay['wu'], lay['wd'])
    return h2.reshape(b, lp, D_MODEL)[:, N_META:seq_len]


def kernel(x_prompt, x_sample, meta_tokens, norm_mix, norm_ffn, w_in, b_gate, q_norm, k_norm, lambda_q1, lambda_k1, lambda_q2, lambda_k2, attn_subln, w_attn_branch, conv_w, conv_b, lru_wa, lru_ba, lru_wx, lru_bx, lru_lambda, w_lru_branch, w_out, ffn_w_gate, ffn_w_up, ffn_w_down, moe_router, moe_w_gate, moe_w_up, moe_w_down):
    p = dict(norm_mix=norm_mix, norm_ffn=norm_ffn, w_in=w_in, b_gate=b_gate, q_norm=q_norm,
             k_norm=k_norm, lambda_q1=lambda_q1, lambda_k1=lambda_k1, lambda_q2=lambda_q2,
             lambda_k2=lambda_k2, attn_subln=attn_subln, w_attn_branch=w_attn_branch,
             conv_w=conv_w, conv_b=conv_b, lru_wa=lru_wa, lru_ba=lru_ba, lru_wx=lru_wx,
             lru_bx=lru_bx, lru_lambda=lru_lambda, w_lru_branch=w_lru_branch, w_out=w_out,
             ffn_w_gate=ffn_w_gate, ffn_w_up=ffn_w_up, ffn_w_down=ffn_w_down,
             moe_router=moe_router, moe_w_gate=moe_w_gate, moe_w_up=moe_w_up,
             moe_w_down=moe_w_down)
    layers = _prepare(p)
    return (_trunk(x_prompt, meta_tokens, layers), _trunk(x_sample, meta_tokens, layers))
```

```python
import functools
import math

import jax
import jax.numpy as jnp
from jax import lax
from jax.experimental import pallas as pl
from jax.experimental.pallas import tpu as pltpu

F32 = jnp.float32
BF16 = jnp.bfloat16

D_MODEL = 1024
N_META = 16
N_HEADS = 8
HEAD_DIM = 64
V_DIM = 2 * HEAD_DIM
LRU_BLOCKS = 16
LRU_BLOCK_W = D_MODEL // LRU_BLOCKS
LRU_C = 8.0
N_EXPERTS = 8
EPS = 1e-6
IN_WIDTH = 7 * D_MODEL

LANE = 128
SUBLANE = 8
VMEM_LIMIT_BYTES = 48 * 1024 * 1024
NEG_BIG = -1e30

ATT_TILE = 512
ATT_MAX_UNROLL = 8
EXP_ZERO_LOGIT = 105.0
VT_ROWS = V_DIM + 16
ATT_TAIL = LANE
ATT_VMEM_LIMIT_BYTES = 56 * 1024 * 1024
MASK_LOGIT = -30000.0
SAFE_LOGIT_SPAN = 80.0

TOP_K = 2
MOE_TM = 1024
MOE_TF = 512
MOE_DMA_UNROLL = 8

LRU_CB = 128
LRU_TT = 128


def _largest_tile(n, target, mult=LANE):
    assert n % mult == 0, (n, mult)
    best = mult
    t = mult
    while t <= min(n, target):
        if n % t == 0:
            best = t
        t += mult
    return best


def _cparams(*sem):
    return pltpu.CompilerParams(dimension_semantics=sem, vmem_limit_bytes=VMEM_LIMIT_BYTES)


def _rmsnorm_rows(x, g):
    ms = jnp.mean(x * x, axis=-1, keepdims=True)
    return x * lax.rsqrt(ms + EPS) * g


def _in_proj_kernel(h_ref, g_ref, w_ref, z_ref, xn_ref):
    @pl.when(pl.program_id(1) == 0)
    def _():
        xn_ref[...] = _rmsnorm_rows(h_ref[...], g_ref[...]).astype(BF16)

    z_ref[...] = jnp.dot(xn_ref[...], w_ref[...], preferred_element_type=F32)


def _in_proj(h2, g, w):
    m = h2.shape[0]
    tm = _largest_tile(m, 1024)
    tn = _largest_tile(IN_WIDTH, 1792, 2 * LANE)
    return pl.pallas_call(
        _in_proj_kernel,
        grid=(m // tm, IN_WIDTH // tn),
        in_specs=[
            pl.BlockSpec((tm, D_MODEL), lambda i, j: (i, 0)),
            pl.BlockSpec((1, D_MODEL), lambda i, j: (0, 0)),
            pl.BlockSpec((D_MODEL, tn), lambda i, j: (0, j)),
        ],
        out_specs=pl.BlockSpec((tm, tn), lambda i, j: (i, j)),
        out_shape=jax.ShapeDtypeStruct((m, IN_WIDTH), F32),
        scratch_shapes=[pltpu.VMEM((tm, D_MODEL), BF16)],
        compiler_params=_cparams("parallel", "arbitrary"),
        name="in_proj",
    )(h2, g, w)


def _attn_kernel(sc_ref, q_ref, k_ref, v_ref, qg_ref, kg_ref, lam_ref, sg_ref, o_ref,
                 kb_ref, vb_ref, m_ref, l_ref, acc_ref, *, seq_len, tq, tk):
    lp = k_ref.shape[0]
    nk = lp // tk
    head = pl.program_id(1)
    qi = pl.program_id(2)
    lane = lax.broadcasted_iota(jnp.int32, (1, LANE), 1)
    lo = lane < HEAD_DIM

    def half_rmsnorm(x, g):
        sq = x * x
        s_lo = jnp.sum(jnp.where(lo, sq, 0.0), axis=-1, keepdims=True)
        s_hi = jnp.sum(jnp.where(lo, 0.0, sq), axis=-1, keepdims=True)
        ms = jnp.where(lo, s_lo, s_hi) * (1.0 / HEAD_DIM)
        return x * lax.rsqrt(ms + EPS) * g

    @pl.when(qi == 0)
    def _():
        def fill(c, carry):
            r = pl.ds(pl.multiple_of(c * tk, tk), tk)
            kb_ref[r, :] = half_rmsnorm(k_ref[r, :], kg_ref[...]).astype(BF16)
            vb_ref[r, :] = v_ref[r, :].astype(BF16)
            return carry
        lax.fori_loop(0, nk, fill, 0)

    qn = half_rmsnorm(q_ref[...], qg_ref[...]) * (1.0 / math.sqrt(HEAD_DIM))
    q_comp = (jnp.where(lo, qn, 0.0).astype(BF16), jnp.where(lo, 0.0, qn).astype(BF16))

    m_ref[...] = jnp.full(m_ref.shape, NEG_BIG, F32)
    l_ref[...] = jnp.zeros(l_ref.shape, F32)
    acc_ref[...] = jnp.zeros(acc_ref.shape, F32)

    slope = sc_ref[head]
    rel = (lax.broadcasted_iota(jnp.int32, (tq, tk), 0)
           - lax.broadcasted_iota(jnp.int32, (tq, tk), 1)).astype(F32)
    col = lax.broadcasted_iota(jnp.int32, (1, tk), 1)

    def chunk(kc, carry):
        k0 = pl.multiple_of(kc * tk, tk)
        kblk = kb_ref[pl.ds(k0, tk), :]
        vblk = vb_ref[pl.ds(k0, tk), :]
        off = (qi * tq - k0).astype(F32)
        bias = -slope * jnp.abs(rel + off) + jnp.where(col + k0 < seq_len, 0.0, NEG_BIG)
        for c in range(2):
            s = lax.dot_general(q_comp[c], kblk, (((1,), (1,)), ((), ())),
                                preferred_element_type=F32) + bias
            m_prev = m_ref[c]
            m_new = jnp.maximum(m_prev, jnp.max(s, axis=-1, keepdims=True))
            alpha = jnp.exp(m_prev - m_new)
            p = jnp.exp(s - m_new)
            l_ref[c] = alpha * l_ref[c] + jnp.sum(p, axis=-1, keepdims=True)
            acc_ref[c] = alpha * acc_ref[c] + jnp.dot(p.astype(BF16), vblk,
                                                      preferred_element_type=F32)
            m_ref[c] = m_new
        return carry

    lax.fori_loop(0, nk, chunk, 0)

    lam_init = sc_ref[N_HEADS]
    lp_ = lam_ref[...]
    lam = (jnp.exp(jnp.sum(lp_[0:1] * lp_[1:2], axis=-1, keepdims=True))
           - jnp.exp(jnp.sum(lp_[2:3] * lp_[3:4], axis=-1, keepdims=True)) + lam_init)
    o = acc_ref[0] / l_ref[0] - lam * (acc_ref[1] / l_ref[1])
    o = _rmsnorm_rows(o, sg_ref[...]) * (1.0 - lam_init)
    o_ref[...] = o.astype(BF16)


def _attn_bounded_kernel(sc_ref, q_ref, k_ref, v_ref, qg_ref, kg_ref, lam_ref, sg_ref, prev_ref,
                         o_ref, kq_ref, vt_ref, vtail_ref, lhs_ref, bias_ref, acc_ref,
                         *, seq_len, t, head0, reach):
    lp = k_ref.shape[0]
    s_main = lp - ATT_TAIL
    n_main = s_main // t
    del prev_ref
    slope = sc_ref[head0 + pl.program_id(1)]
    lam_init = sc_ref[N_HEADS]
    lane = lax.broadcasted_iota(jnp.int32, (1, LANE), 1)
    lo = lane < HEAD_DIM
    nt = (((1,), (1,)), ((), ()))

    def half_rmsnorm(x, g):
        sq = x * x
        s_lo = jnp.sum(jnp.where(lo, sq, 0.0), axis=-1, keepdims=True)
        s_hi = jnp.sum(jnp.where(lo, 0.0, sq), axis=-1, keepdims=True)
        ms = jnp.where(lo, s_lo, s_hi) * (1.0 / HEAD_DIM)
        return x * lax.rsqrt(ms + EPS) * g

    def split2(x):
        hi = x.astype(BF16).astype(F32)
        return hi, x - hi

    def pos_col(r0, n):
        return lax.broadcasted_iota(jnp.int32, (n, 1), 0) + r0

    lp_ = lam_ref[...]
    lam = (jnp.exp(jnp.sum(lp_[0:1] * lp_[1:2], axis=-1, keepdims=True))
           - jnp.exp(jnp.sum(lp_[2:3] * lp_[3:4], axis=-1, keepdims=True)) + lam_init)

    ones_row = jnp.where(lax.broadcasted_iota(jnp.int32, (VT_ROWS - V_DIM, 1), 0) == 0, 1.0, 0.0)

    def fill(r0, n, vt_dst):
        rows = pos_col(r0, n)
        valid = rows < seq_len
        kn = jnp.where(valid, half_rmsnorm(k_ref[pl.ds(r0, n), :], kg_ref[...]), 0.0)
        sj_hi, sj_lo = split2(slope * rows.astype(F32))
        aug = jnp.where(lane == 0, -1.0,
              jnp.where((lane == 1) | (lane == 2), 1.0,
              jnp.where(lane == 3, sj_hi,
              jnp.where(lane == 4, sj_lo,
              jnp.where(lane == 5, jnp.where(valid, 0.0, MASK_LOGIT), 0.0)))))
        kq_ref[pl.ds(r0, n), 0:LANE] = kn.astype(BF16)
        kq_ref[pl.ds(r0, n), LANE:2 * LANE] = aug.astype(BF16)
        vt = jnp.transpose(jnp.where(valid, v_ref[pl.ds(r0, n), :], 0.0))
        vt_dst[...] = jnp.concatenate(
            [vt, jnp.broadcast_to(ones_row, (VT_ROWS - V_DIM, n))], axis=0).astype(BF16)

    def fill_chunk(c, carry):
        fill(pl.multiple_of(c * t, t), t, vt_ref.at[c])
        return carry
    lax.fori_loop(0, n_main, fill_chunk, 0, unroll=2)
    fill(s_main, ATT_TAIL, vtail_ref)

    def comp_max(g, c):
        return jnp.max(jnp.where(lo == (c == 0), jnp.abs(g), 0.0), axis=-1, keepdims=True)
    bounds = [math.sqrt(HEAD_DIM) * comp_max(qg_ref[...], c) * comp_max(kg_ref[...], c)
              for c in range(2)]

    rel = (lax.broadcasted_iota(jnp.int32, (t, t), 0)
           - lax.broadcasted_iota(jnp.int32, (t, t), 1)).astype(F32)
    bias_ref[0] = jnp.zeros((t, t), F32)
    bias_ref[1] = -slope * jnp.abs(rel)

    def build_lhs(r0, n):
        rows = pos_col(r0, n)
        qn = half_rmsnorm(q_ref[pl.ds(r0, n), :], qg_ref[...]) * (1.0 / math.sqrt(HEAD_DIM))
        si_hi, si_lo = split2(slope * rows.astype(F32))
        side = jnp.where(lane == 1, si_hi,
               jnp.where(lane == 2, si_lo,
               jnp.where((lane == 3) | (lane == 4), -1.0, 0.0)))
        for c in range(2):
            qc = jnp.where(lo, qn, 0.0) if c == 0 else jnp.where(lo, 0.0, qn)
            base = jnp.broadcast_to(
                jnp.where(lane == 0, bounds[c], jnp.where(lane == 5, 1.0, 0.0)), (n, LANE))
            qcb = qc.astype(BF16)
            for x, aug in enumerate((base - side, base, base + side)):
                lhs_ref[x, c * n:(c + 1) * n, 0:LANE] = qcb
                lhs_ref[x, c * n:(c + 1) * n, LANE:2 * LANE] = aug.astype(BF16)

    def process(x, n, k0, nk, vt, bias):
        s = lax.dot_general(kq_ref[pl.ds(k0, nk), :], lhs_ref[x, 0:2 * n, :], nt,
                            preferred_element_type=F32)
        if bias is not None:
            s = s + jnp.concatenate([bias, bias], axis=1)
        p = jnp.exp(s).astype(BF16)
        acc_ref[:, 0:2 * n] += jnp.dot(vt, p, preferred_element_type=F32)

    def finalize(r0, n):
        a0 = acc_ref[:, 0:n]
        a1 = acc_ref[:, n:2 * n]
        o = (a0[0:V_DIM] * (1.0 / a0[V_DIM:V_DIM + 1])
             - (lam * (1.0 / a1[V_DIM:V_DIM + 1])) * a1[0:V_DIM])
        ms = jnp.mean(o * o, axis=0, keepdims=True)
        o = o * lax.rsqrt(ms + EPS) * sg_ref[...] * (1.0 - lam_init)
        pos = lax.broadcasted_iota(jnp.int32, (1, n), 1) + r0
        o = jnp.where(pos < seq_len, o, 0.0)
        o_ref[pl.ds(r0, n), :] = jnp.transpose(o).astype(BF16)

    window = min(2 * reach + 1, n_main)

    def full_chunks(qi, n, kc0, count):
        def body(w, carry):
            kc = kc0 + w
            x = jnp.where(kc < qi, 0, jnp.where(kc == qi, 1, 2))
            d = jnp.where(kc == qi, 1, 0)
            process(x, n, pl.multiple_of(kc * t, t), t, vt_ref[kc], bias_ref[d, :, 0:n])
            return carry
        lax.fori_loop(0, count, body, 0, unroll=min(count, ATT_MAX_UNROLL))

    def q_main(qi, carry):
        r0 = pl.multiple_of(qi * t, t)
        build_lhs(r0, t)
        acc_ref[...] = jnp.zeros(acc_ref.shape, F32)
        full_chunks(qi, t, jnp.clip(qi - reach, 0, n_main - window), window)

        @pl.when(n_main - qi <= reach)
        def _():
            process(2, t, s_main, ATT_TAIL, vtail_ref[...], None)
        finalize(r0, t)
        return carry
    lax.fori_loop(0, n_main, q_main, 0)

    build_lhs(s_main, ATT_TAIL)
    acc_ref[...] = jnp.zeros(acc_ref.shape, F32)
    tail_count = min(reach, n_main)
    full_chunks(n_main, ATT_TAIL, n_main - tail_count, tail_count)
    process(1, ATT_TAIL, s_main, ATT_TAIL, vtail_ref[...], bias_ref[1, 0:ATT_TAIL, 0:ATT_TAIL])
    finalize(s_main, ATT_TAIL)


def _chunk_reach(head, t):
    slope = 2.0 ** (-8.0 * (head + 1) / N_HEADS)
    return max(1, math.ceil((EXP_ZERO_LOGIT / slope - 1.0) / t))


def _attention_bounded(z3, scal, qg, kg, lamp, sg, seq_len):
    b, lp, _ = z3.shape
    t = _largest_tile(lp - ATT_TAIL, ATT_TILE)
    n_main = (lp - ATT_TAIL) // t
    groups = []
    for head in range(N_HEADS):
        reach = min(_chunk_reach(head, t), n_main)
        if groups and min(2 * groups[-1][2] + 1, n_main) == min(2 * reach + 1, n_main):
            groups[-1][1] += 1
            groups[-1][2] = max(groups[-1][2], reach)
        else:
            groups.append([head, 1, reach])
    small = lambda shape: pl.BlockSpec(shape, lambda bi, hi: (0, 0))
    att = jnp.zeros((b, lp, N_HEADS * V_DIM), BF16)
    for head0, count, reach in groups:
        kern = functools.partial(_attn_bounded_kernel, seq_len=seq_len, t=t, head0=head0,
                                 reach=reach)
        col = lambda off, head0=head0: pl.BlockSpec(
            (None, lp, V_DIM), lambda bi, hi: (bi, 0, off + head0 + hi))
        att = pl.pallas_call(
            kern,
            grid=(b, count),
            in_specs=[
                pl.BlockSpec(memory_space=pltpu.SMEM),
                col(0), col(N_HEADS), col(2 * N_HEADS),
                small((1, V_DIM)), small((1, V_DIM)), small((4, HEAD_DIM)), small((V_DIM, 1)),
                pl.BlockSpec(memory_space=pl.ANY),
            ],
            out_specs=col(0),
            out_shape=jax.ShapeDtypeStruct(att.shape, att.dtype),
            input_output_aliases={8: 0},
            scratch_shapes=[
                pltpu.VMEM((lp, 2 * LANE), BF16),
                pltpu.VMEM((n_main, VT_ROWS, t), BF16), pltpu.VMEM((VT_ROWS, ATT_TAIL), BF16),
                pltpu.VMEM((3, 2 * t, 2 * LANE), BF16),
                pltpu.VMEM((2, t, t), F32),
                pltpu.VMEM((VT_ROWS, 2 * t), F32),
            ],
            compiler_params=pltpu.CompilerParams(dimension_semantics=("parallel", "parallel"),
                                                 vmem_limit_bytes=ATT_VMEM_LIMIT_BYTES),
            name="diff_attention_bounded",
        )(scal, z3, z3, z3, qg, kg, lamp, sg.reshape(V_DIM, 1), att)
    return att


def _attention(z3, scal, qg, kg, lamp, sg, seq_len):
    span = 2.0 * HEAD_DIM / math.sqrt(HEAD_DIM) * jnp.max(jnp.abs(qg)) * jnp.max(jnp.abs(kg))
    args = (z3, scal, qg, kg, lamp, sg)
    return lax.cond(span <= SAFE_LOGIT_SPAN,
                    lambda *a: _attention_bounded(*a, seq_len),
                    lambda *a: _attention_online(*a, seq_len), *args)


def _attention_online(z3, scal, qg, kg, lamp, sg, seq_len):
    b, lp, _ = z3.shape
    tq = _largest_tile(lp, 768)
    tk = tq
    kern = functools.partial(_attn_kernel, seq_len=seq_len, tq=tq, tk=tk)
    small = lambda shape: pl.BlockSpec(shape, lambda bi, hi, qi: (0, 0))
    return pl.pallas_call(
        kern,
        grid=(b, N_HEADS, lp // tq),
        in_specs=[
            pl.BlockSpec(memory_space=pltpu.SMEM),
            pl.BlockSpec((None, tq, V_DIM), lambda bi, hi, qi: (bi, qi, hi)),
            pl.BlockSpec((None, lp, V_DIM), lambda bi, hi, qi: (bi, 0, N_HEADS + hi)),
            pl.BlockSpec((None, lp, V_DIM), lambda bi, hi, qi: (bi, 0, 2 * N_HEADS + hi)),
            small((1, V_DIM)), small((1, V_DIM)), small((4, HEAD_DIM)), small((1, V_DIM)),
        ],
        out_specs=pl.BlockSpec((None, tq, V_DIM), lambda bi, hi, qi: (bi, qi, hi)),
        out_shape=jax.ShapeDtypeStruct((b, lp, N_HEADS * V_DIM), BF16),
        scratch_shapes=[
            pltpu.VMEM((lp, V_DIM), BF16), pltpu.VMEM((lp, V_DIM), BF16),
            pltpu.VMEM((2, tq, 1), F32), pltpu.VMEM((2, tq, 1), F32),
            pltpu.VMEM((2, tq, V_DIM), F32),
        ],
        compiler_params=_cparams("parallel", "parallel", "arbitrary"),
        name="diff_attention",
    )(scal, z3, z3, z3, qg, kg, lamp, sg)


def _lru_kernel(xl_ref, gl_ref, cw_ref, cbias_ref, w_ref, ba_ref, bx_ref, lam_ref, o_ref,
                xp_ref, hf_ref, hb_ref, *, seq_len, tt):
    lp, cb = xl_ref.shape
    n = lp // tt
    pad = 8
    rows_t = lax.broadcasted_iota(jnp.int32, (tt, 1), 0)

    xp_ref[0:pad, :] = jnp.zeros((pad, cb), F32)
    xp_ref[lp + pad:lp + 2 * pad, :] = jnp.zeros((pad, cb), F32)

    def copy_in(c, carry):
        r0 = pl.multiple_of(c * tt, tt)
        x = xl_ref[pl.ds(r0, tt), :]
        xp_ref[pl.ds(r0 + pad, tt), :] = jnp.where(rows_t + r0 < seq_len, x, 0.0)
        return carry
    lax.fori_loop(0, n, copy_in, 0)

    cw = cw_ref[...]

    def gates(c, d):
        r0 = pl.multiple_of(c * tt, tt)
        win = xp_ref[pl.ds(r0, tt + 2 * pad), :]
        wn = tt + 2 * pad
        xc = (pltpu.roll(win, 2, 0)[pad:pad + tt] * cw[0:1]
              + pltpu.roll(win, 1, 0)[pad:pad + tt] * cw[1:2]
              + win[pad:pad + tt] * cw[2:3]
              + pltpu.roll(win, wn - 1, 0)[pad:pad + tt] * cw[3:4]
              + cbias_ref[...])
        pre = jnp.dot(xc.astype(BF16), w_ref[:, d * 2 * cb:(d + 1) * 2 * cb],
                      preferred_element_type=F32)
        r = jax.nn.sigmoid(pre[:, :cb] + ba_ref[d:d + 1, :])
        i = jax.nn.sigmoid(pre[:, cb:] + bx_ref[d:d + 1, :])
        nl = -lam_ref[d:d + 1, :]
        softplus = jnp.maximum(nl, 0.0) + jnp.log1p(jnp.exp(-jnp.abs(nl)))
        log_a = (-LRU_C * r) * softplus
        a = jnp.exp(log_a)
        th = jnp.tanh(log_a)
        m2 = -2.0 * th / (1.0 - th)
        mult = jnp.where(m2 > 0.0, m2 * lax.rsqrt(m2), 0.0)
        u = mult * (i * xc)
        u = jnp.where(rows_t + r0 < seq_len, u, 0.0)
        return a, u

    ng = tt // SUBLANE
    sub = lax.broadcasted_iota(jnp.int32, (1, SUBLANE, 1), 1)

    def scan_chunk(a, u, carry, reverse):
        a3 = a.reshape(ng, SUBLANE, cb)
        u3 = u.reshape(ng, SUBLANE, cb)
        d = 1
        while d < SUBLANE:
            if reverse:
                shift, keep = SUBLANE - d, sub < SUBLANE - d
            else:
                shift, keep = d, sub >= d
            u3 = u3 + a3 * jnp.where(keep, pltpu.roll(u3, shift, 1), 0.0)
            a3 = a3 * jnp.where(keep, pltpu.roll(a3, shift, 1), 1.0)
            d *= 2
        hs = [None] * ng
        for g in (range(ng - 1, -1, -1) if reverse else range(ng)):
            hg = u3[g] + a3[g] * carry
            hs[g] = hg
            carry = hg[0:1, :] if reverse else hg[SUBLANE - 1:SUBLANE, :]
        return jnp.concatenate(hs, axis=0), carry

    def both(j, carry):
        cf = j
        h, state_f = scan_chunk(*gates(cf, 0), carry[0], False)
        hf_ref[pl.ds(pl.multiple_of(cf * tt, tt), tt), :] = h
        cr = n - 1 - j
        h, state_b = scan_chunk(*gates(cr, 1), carry[1], True)
        hb_ref[pl.ds(pl.multiple_of(cr * tt, tt), tt), :] = h
        return state_f, state_b
    zero_state = jnp.zeros((1, cb), F32)
    lax.fori_loop(0, n, both, (zero_state, zero_state), unroll=2)

    def combine(c, carry):
        r = pl.ds(pl.multiple_of(c * tt, tt), tt)
        o_ref[r, :] = ((hf_ref[r, :] + hb_ref[r, :]) * jax.nn.gelu(gl_ref[r, :])).astype(BF16)
        return carry
    lax.fori_loop(0, n, combine, 0, unroll=2)


def _lru(z3, cw, cbias, wcat, ba, bx, lam, seq_len):
    b, lp, _ = z3.shape
    cb = LRU_CB
    nc = D_MODEL // cb
    kern = functools.partial(_lru_kernel, seq_len=seq_len, tt=LRU_TT)
    chan = lambda rows: pl.BlockSpec((rows, cb), lambda bi, ci: (0, ci))
    return pl.pallas_call(
        kern,
        grid=(b, nc),
        in_specs=[
            pl.BlockSpec((None, lp, cb), lambda bi, ci: (bi, 0, 3 * nc + ci)),
            pl.BlockSpec((None, lp, cb), lambda bi, ci: (bi, 0, 4 * nc + ci)),
            chan(4), chan(1),
            pl.BlockSpec((None, cb, 4 * cb), lambda bi, ci: (ci, 0, 0)),
            chan(2), chan(2), chan(2),
        ],
        out_specs=pl.BlockSpec((None, lp, cb), lambda bi, ci: (bi, 0, ci)),
        out_shape=jax.ShapeDtypeStruct((b, lp, D_MODEL), BF16),
        scratch_shapes=[pltpu.VMEM((lp + 16, cb), F32), pltpu.VMEM((lp, cb), F32),
                        pltpu.VMEM((lp, cb), F32)],
        compiler_params=_cparams("parallel", "parallel"),
        name="rglru",
    )(z3, z3, cw, cbias, wcat, ba, bx, lam)


def _merge_kernel(att_ref, yl_ref, ga_ref, gr_ref, h_ref, wa_ref, wl_ref, wo_ref,
                  bga_ref, bgr_ref, o_ref):
    ya = jnp.dot(att_ref[...], wa_ref[...], preferred_element_type=F32)
    yr = jnp.dot(yl_ref[...], wl_ref[...], preferred_element_type=F32)
    merged = (jax.nn.sigmoid(ga_ref[...] + bga_ref[...]) * ya
              + jax.nn.sigmoid(gr_ref[...] + bgr_ref[...]) * yr)
    o_ref[...] = h_ref[...] + jnp.dot(merged.astype(BF16), wo_ref[...],
                                      preferred_element_type=F32)


def _merge(att2, yl2, z2, h2, wa, wl, wo, bga, bgr):
    m = h2.shape[0]
    tm = _largest_tile(m, 512)
    row = lambda col: pl.BlockSpec((tm, D_MODEL), lambda i: (i, col))
    full = lambda shape: pl.BlockSpec(shape, lambda i: (0, 0))
    return pl.pallas_call(
        _merge_kernel,
        grid=(m // tm,),
        in_specs=[row(0), row(0), row(5), row(6), row(0),
                  full((D_MODEL, D_MODEL)), full((D_MODEL, D_MODEL)), full((D_MODEL, D_MODEL)),
                  full((1, D_MODEL)), full((1, D_MODEL))],
        out_specs=row(0),
        out_shape=jax.ShapeDtypeStruct((m, D_MODEL), F32),
        compiler_params=_cparams("parallel"),
        name="merge_out_proj",
    )(att2, yl2, z2, z2, h2, wa, wl, wo, bga, bgr)


def _ffn_kernel(*refs, chunks_per_expert):
    if chunks_per_expert:
        h_ref, g_ref, wr_ref, wg_ref, wu_ref, wd_ref, o_ref, hn_ref, acc_ref, gate_ref = refs
    else:
        h_ref, g_ref, wg_ref, wu_ref, wd_ref, o_ref, hn_ref, acc_ref = refs
    j = pl.program_id(1)

    @pl.when(j == 0)
    def _():
        h = h_ref[...]
        hn = _rmsnorm_rows(h, g_ref[...])
        hn_ref[...] = hn.astype(BF16)
        acc_ref[...] = h
        if chunks_per_expert:
            logits = jnp.dot(hn, wr_ref[...], preferred_element_type=F32,
                             precision=lax.Precision.HIGHEST)
            lane = lax.broadcasted_iota(jnp.int32, logits.shape, 1).astype(F32)
            lg = jnp.where(lane < N_EXPERTS, logits, NEG_BIG)
            m1 = jnp.max(lg, axis=-1, keepdims=True)
            i1 = jnp.min(jnp.where(lg == m1, lane, float(LANE)), axis=-1, keepdims=True)
            lg2 = jnp.where(lane == i1, NEG_BIG, lg)
            m2 = jnp.max(lg2, axis=-1, keepdims=True)
            i2 = jnp.min(jnp.where(lg2 == m2, lane, float(LANE)), axis=-1, keepdims=True)
            e2 = jnp.exp(m2 - m1)
            den = 1.0 + e2
            gate_ref[...] = jnp.where(lane == i1, 1.0 / den,
                                      jnp.where(lane == i2, e2 / den, 0.0))

    hn = hn_ref[...]
    gg = jnp.dot(hn, wg_ref[...], preferred_element_type=F32)
    uu = jnp.dot(hn, wu_ref[...], preferred_element_type=F32)
    act = (jax.nn.silu(gg) * uu).astype(BF16)
    y = jnp.dot(act, wd_ref[...], preferred_element_type=F32)
    if chunks_per_expert:
        e = (j // chunks_per_expert).astype(F32)
        gates = gate_ref[...]
        lane = lax.broadcasted_iota(jnp.int32, gates.shape, 1).astype(F32)
        y = y * jnp.sum(jnp.where(lane == e, gates, 0.0), axis=-1, keepdims=True)
    acc_ref[...] += y

    @pl.when(j == pl.num_programs(1) - 1)
    def _():
        o_ref[...] = acc_ref[...]


def _ffn(h2, g, wg, wu, wd, router=None, d_ff_expert=None):
    m = h2.shape[0]
    f = wg.shape[1]
    tm = _largest_tile(m, 1024)
    tf = 512
    in_specs = [pl.BlockSpec((tm, D_MODEL), lambda i, j: (i, 0)),
                pl.BlockSpec((1, D_MODEL), lambda i, j: (0, 0))]
    args = [h2, g]
    scratch = [pltpu.VMEM((tm, D_MODEL), BF16), pltpu.VMEM((tm, D_MODEL), F32)]
    cpe = 0
    if router is not None:
        assert d_ff_expert % tf == 0
        cpe = d_ff_expert // tf
        in_specs.append(pl.BlockSpec((D_MODEL, LANE), lambda i, j: (0, 0)))
        args.append(router)
        scratch.append(pltpu.VMEM((tm, LANE), F32))
    in_specs += [pl.BlockSpec((D_MODEL, tf), lambda i, j: (0, j)),
                 pl.BlockSpec((D_MODEL, tf), lambda i, j: (0, j)),
                 pl.BlockSpec((tf, D_MODEL), lambda i, j: (j, 0))]
    args += [wg, wu, wd]
    return pl.pallas_call(
        functools.partial(_ffn_kernel, chunks_per_expert=cpe),
        grid=(m // tm, f // tf),
        in_specs=in_specs,
        out_specs=pl.BlockSpec((tm, D_MODEL), lambda i, j: (i, 0)),
        out_shape=jax.ShapeDtypeStruct((m, D_MODEL), F32),
        scratch_shapes=scratch,
        compiler_params=_cparams("parallel", "arbitrary"),
        name="moe_ffn" if cpe else "dense_ffn",
    )(*args)


def _moe_router_kernel(h_ref, g_ref, wr_ref, r_ref, cnt_ref, tri_ref, base_ref):
    tm = h_ref.shape[0]

    @pl.when(pl.program_id(0) == 0)
    def _():
        row = lax.broadcasted_iota(jnp.int32, (tm, tm), 0)
        col = lax.broadcasted_iota(jnp.int32, (tm, tm), 1)
        tri_ref[...] = jnp.where(col < row, 1.0, 0.0).astype(BF16)
        base_ref[...] = jnp.zeros(base_ref.shape, F32)

    hn = _rmsnorm_rows(h_ref[...], g_ref[...])
    logits = jnp.dot(hn, wr_ref[...], preferred_element_type=F32, precision=lax.Precision.HIGHEST)
    lane = lax.broadcasted_iota(jnp.int32, logits.shape, 1).astype(F32)
    lg = jnp.where(lane < N_EXPERTS, logits, NEG_BIG)
    m1 = jnp.max(lg, axis=-1, keepdims=True)
    i1 = jnp.min(jnp.where(lg == m1, lane, float(LANE)), axis=-1, keepdims=True)
    lg2 = jnp.where(lane == i1, NEG_BIG, lg)
    m2 = jnp.max(lg2, axis=-1, keepdims=True)
    i2 = jnp.min(jnp.where(lg2 == m2, lane, float(LANE)), axis=-1, keepdims=True)
    e2 = jnp.exp(m2 - m1)
    den = 1.0 + e2
    oh1 = lane == i1
    oh2 = lane == i2
    assigned = jnp.where(oh1 | oh2, 1.0, 0.0)
    before = jnp.dot(tri_ref[...], assigned.astype(BF16), preferred_element_type=F32) + base_ref[...]
    rank1 = jnp.sum(jnp.where(oh1, before, 0.0), axis=-1, keepdims=True)
    rank2 = jnp.sum(jnp.where(oh2, before, 0.0), axis=-1, keepdims=True)
    r_ref[...] = jnp.where(lane == 0, i1, jnp.where(lane == 1, i2,
                 jnp.where(lane == 2, 1.0 / den, jnp.where(lane == 3, e2 / den,
                 jnp.where(lane == 4, rank1, jnp.where(lane == 5, rank2, 0.0))))))
    base_ref[...] += jnp.sum(assigned, axis=0, keepdims=True)
    cnt_ref[...] = base_ref[...]


def _moe_router(h2, g, router):
    m = h2.shape[0]
    tm = _largest_tile(m, 1024)
    return pl.pallas_call(
        _moe_router_kernel,
        grid=(m // tm,),
        in_specs=[pl.BlockSpec((tm, D_MODEL), lambda i: (i, 0)),
                  pl.BlockSpec((1, D_MODEL), lambda i: (0, 0)),
                  pl.BlockSpec((D_MODEL, LANE), lambda i: (0, 0))],
        out_specs=[pl.BlockSpec((tm, LANE), lambda i: (i, 0)),
                   pl.BlockSpec((1, LANE), lambda i: (0, 0))],
        out_shape=[jax.ShapeDtypeStruct((m, LANE), F32), jax.ShapeDtypeStruct((1, LANE), F32)],
        scratch_shapes=[pltpu.VMEM((tm, tm), BF16), pltpu.VMEM((1, LANE), F32)],
        compiler_params=_cparams("arbitrary"),
        name="moe_router",
    )(h2, g, router)


def _slab_copy(src_ref, src_tok, dst_ref, dst_tok, sem):
    return pltpu.make_async_copy(
        src_ref.at[pl.ds(pl.multiple_of(src_tok * SUBLANE, SUBLANE), SUBLANE), :],
        dst_ref.at[pl.ds(pl.multiple_of(dst_tok * SUBLANE, SUBLANE), SUBLANE), :], sem)


def _rows_to_slabs(x, slab_ref):
    tm = x.shape[0]
    for s in range(D_MODEL // LANE):
        slab_ref[pl.ds(s, tm, stride=SUBLANE), :] = x[:, s * LANE:(s + 1) * LANE]


def _slabs_lane_block(slab_ref, s, tm):
    return slab_ref[pl.ds(s, tm, stride=SUBLANE), :]


def _moe_scatter_kernel(pos_ref, h_ref, g_ref, xs_in_ref, xs_ref, slab_ref, sem):
    del xs_in_ref
    tm = h_ref.shape[0]
    _rows_to_slabs(_rmsnorm_rows(h_ref[...], g_ref[...]), slab_ref)

    def body(r, carry):
        for s in range(TOP_K):
            _slab_copy(slab_ref, r, xs_ref, pos_ref[s, r], sem).start(priority=s)
        return carry
    lax.fori_loop(0, tm, body, 0, unroll=MOE_DMA_UNROLL)
    for s in range(TOP_K):
        pltpu.make_async_copy(slab_ref, xs_ref.at[pl.ds(0, tm * SUBLANE), :], sem).wait()


def _moe_scatter(pos, h2, g, n_rows):
    m = h2.shape[0]
    tm = _largest_tile(m, MOE_TM)
    xs0 = jnp.zeros((n_rows * SUBLANE, LANE), F32)
    return pl.pallas_call(
        _moe_scatter_kernel,
        grid=(m // tm,),
        in_specs=[pl.BlockSpec((TOP_K, tm), lambda i: (0, i), memory_space=pltpu.SMEM),
                  pl.BlockSpec((tm, D_MODEL), lambda i: (i, 0)),
                  pl.BlockSpec((1, D_MODEL), lambda i: (0, 0)),
                  pl.BlockSpec(memory_space=pl.ANY)],
        out_specs=pl.BlockSpec(memory_space=pl.ANY),
        out_shape=jax.ShapeDtypeStruct(xs0.shape, xs0.dtype),
        input_output_aliases={3: 0},
        scratch_shapes=[pltpu.VMEM((tm * SUBLANE, LANE), F32), pltpu.SemaphoreType.DMA(())],
        compiler_params=_cparams("arbitrary"),
        name="moe_scatter",
    )(pos, h2, g, xs0)


def _moe_expert_kernel(te_ref, x_ref, wg_ref, wu_ref, wd_ref, y_ref, xb_ref, acc_ref):
    del te_ref
    j = pl.program_id(1)
    tm = xb_ref.shape[0]

    @pl.when(j == 0)
    def _():
        for s in range(D_MODEL // LANE):
            xb_ref[:, s * LANE:(s + 1) * LANE] = _slabs_lane_block(x_ref, s, tm).astype(BF16)
        acc_ref[...] = jnp.zeros(acc_ref.shape, F32)

    xb = xb_ref[...]
    gg = jnp.dot(xb, wg_ref[...], preferred_element_type=F32)
    uu = jnp.dot(xb, wu_ref[...], preferred_element_type=F32)
    act = (jax.nn.silu(gg) * uu).astype(BF16)
    acc_ref[...] += jnp.dot(act, wd_ref[...], preferred_element_type=F32)

    @pl.when(j == pl.num_programs(1) - 1)
    def _():
        _rows_to_slabs(acc_ref[...], y_ref)


def _moe_experts(tile_expert, xs, wg, wu, wd):
    n_rows = xs.shape[0] // SUBLANE
    fe = wg.shape[2]
    tm, tf = MOE_TM, MOE_TF
    slab_tile = pl.BlockSpec((tm * SUBLANE, LANE), lambda i, j, te: (i, 0))
    grid_spec = pltpu.PrefetchScalarGridSpec(
        num_scalar_prefetch=1,
        grid=(n_rows // tm, fe // tf),
        in_specs=[slab_tile,
                  pl.BlockSpec((None, D_MODEL, tf), lambda i, j, te: (te[i], 0, j)),
                  pl.BlockSpec((None, D_MODEL, tf), lambda i, j, te: (te[i], 0, j)),
                  pl.BlockSpec((None, tf, D_MODEL), lambda i, j, te: (te[i], j, 0))],
        out_specs=slab_tile,
        scratch_shapes=[pltpu.VMEM((tm, D_MODEL), BF16), pltpu.VMEM((tm, D_MODEL), F32)])
    return pl.pallas_call(
        _moe_expert_kernel,
        grid_spec=grid_spec,
        out_shape=jax.ShapeDtypeStruct(xs.shape, F32),
        compiler_params=_cparams("parallel", "arbitrary"),
        name="moe_experts",
    )(tile_expert, xs, wg, wu, wd)


def _moe_combine_kernel(pos_ref, r_ref, h_ref, ys_ref, o_ref, y0_ref, y1_ref, sem):
    tm = h_ref.shape[0]
    bufs = (y0_ref, y1_ref)

    def body(r, carry):
        for s in range(TOP_K):
            _slab_copy(ys_ref, pos_ref[s, r], bufs[s], r, sem).start(priority=s)
        return carry
    lax.fori_loop(0, tm, body, 0, unroll=MOE_DMA_UNROLL)
    for s in range(TOP_K):
        pltpu.make_async_copy(ys_ref.at[pl.ds(0, tm * SUBLANE), :], bufs[s], sem).wait()
    r = r_ref[...]
    for s in range(D_MODEL // LANE):
        o_ref[:, s * LANE:(s + 1) * LANE] = h_ref[:, s * LANE:(s + 1) * LANE] + (
            r[:, 2:3] * _slabs_lane_block(y0_ref, s, tm) + r[:, 3:4] * _slabs_lane_block(y1_ref, s, tm))


def _moe_combine(pos, route, h2, ys):
    m = h2.shape[0]
    tm = _largest_tile(m, MOE_TM)
    return pl.pallas_call(
        _moe_combine_kernel,
        grid=(m // tm,),
        in_specs=[pl.BlockSpec((TOP_K, tm), lambda i: (0, i), memory_space=pltpu.SMEM),
                  pl.BlockSpec((tm, LANE), lambda i: (i, 0)),
                  pl.BlockSpec((tm, D_MODEL), lambda i: (i, 0)),
                  pl.BlockSpec(memory_space=pl.ANY)],
        out_specs=pl.BlockSpec((tm, D_MODEL), lambda i: (i, 0)),
        out_shape=jax.ShapeDtypeStruct((m, D_MODEL), F32),
        scratch_shapes=[pltpu.VMEM((tm * SUBLANE, LANE), F32), pltpu.VMEM((tm * SUBLANE, LANE), F32),
                        pltpu.SemaphoreType.DMA(())],
        compiler_params=_cparams("arbitrary"),
        name="moe_combine",
    )(pos, route, h2, ys)


def _moe(h2, g, router, wg, wu, wd):
    m = h2.shape[0]
    route, counts = _moe_router(h2, g, router)
    cnt = counts[0, :N_EXPERTS].astype(jnp.int32)
    padded = (cnt + MOE_TM - 1) // MOE_TM * MOE_TM
    ends = jnp.cumsum(padded)
    starts = ends - padded
    expert = route[:, 0:TOP_K].astype(jnp.int32)
    pos = (starts[expert] + route[:, 4:4 + TOP_K].astype(jnp.int32)).T
    n_rows = TOP_K * m + N_EXPERTS * MOE_TM
    tile_start = jnp.arange(n_rows // MOE_TM, dtype=jnp.int32) * MOE_TM
    tile_expert = jnp.minimum(jnp.searchsorted(ends, tile_start, side='right'),
                              N_EXPERTS - 1).astype(jnp.int32)
    xs = _moe_scatter(pos, h2, g, n_rows)
    ys = _moe_experts(tile_expert, xs, wg, wu, wd)
    return _moe_combine(pos, route, h2, ys)


def _blockdiag_groups(w, per):
    g = LRU_BLOCKS // per
    w = w.reshape(g, per, LRU_BLOCK_W, LRU_BLOCK_W)
    eye = jnp.eye(per, dtype=w.dtype)
    out = jnp.einsum('gpio,pq->gpiqo', w, eye)
    return out.reshape(g, per * LRU_BLOCK_W, per * LRU_BLOCK_W)


def _prepare(p):
    depth = p['w_in'].shape[0]
    per = LRU_CB // LRU_BLOCK_W
    layers = []
    slopes = jnp.exp2(-8.0 * jnp.arange(1, N_HEADS + 1, dtype=F32) / N_HEADS)
    for l in range(depth):
        lam_init = 0.8 - 0.6 * math.exp(-0.3 * l)
        wcat = jnp.concatenate(
            [_blockdiag_groups(p['lru_wa'][l, 0], per), _blockdiag_groups(p['lru_wx'][l, 0], per),
             _blockdiag_groups(p['lru_wa'][l, 1], per), _blockdiag_groups(p['lru_wx'][l, 1], per)],
            axis=-1).astype(BF16)
        lay = dict(
            norm_mix=p['norm_mix'][l][None], norm_ffn=p['norm_ffn'][l][None],
            w_in=p['w_in'][l].astype(BF16),
            scal=jnp.concatenate([slopes, jnp.full((N_HEADS,), lam_init, F32)]),
            qg=jnp.tile(p['q_norm'][l], 2)[None], kg=jnp.tile(p['k_norm'][l], 2)[None],
            lamp=jnp.stack([p['lambda_q1'][l], p['lambda_k1'][l],
                            p['lambda_q2'][l], p['lambda_k2'][l]]),
            sg=p['attn_subln'][l][None],
            conv_w=p['conv_w'][l], conv_b=p['conv_b'][l][None], wcat=wcat,
            ba=p['lru_ba'][l], bx=p['lru_bx'][l], lam=p['lru_lambda'][l],
            wa=p['w_attn_branch'][l].astype(BF16), wl=p['w_lru_branch'][l].astype(BF16),
            wo=p['w_out'][l].astype(BF16),
            bga=p['b_gate'][l, :D_MODEL][None], bgr=p['b_gate'][l, D_MODEL:][None],
        )
        j = l // 2
        if l % 2 == 0:
            lay.update(wg=p['ffn_w_gate'][j].astype(BF16), wu=p['ffn_w_up'][j].astype(BF16),
                       wd=p['ffn_w_down'][j].astype(BF16))
        else:
            lay.update(
                wg=p['moe_w_gate'][j].astype(BF16), wu=p['moe_w_up'][j].astype(BF16),
                wd=p['moe_w_down'][j].astype(BF16),
                router=jnp.pad(p['moe_router'][j], ((0, 0), (0, LANE - N_EXPERTS))))
        layers.append(lay)
    return layers


def _trunk(x, meta, layers):
    b, s, _ = x.shape
    seq_len = s + N_META
    lp = -(-seq_len // LANE) * LANE
    h = jnp.concatenate(
        [jnp.broadcast_to(meta[None].astype(x.dtype), (b, N_META, D_MODEL)), x,
         jnp.zeros((b, lp - seq_len, D_MODEL), x.dtype)], axis=1)
    h2 = h.reshape(b * lp, D_MODEL)
    for lay in layers:
        z2 = _in_proj(h2, lay['norm_mix'], lay['w_in'])
        z3 = z2.reshape(b, lp, IN_WIDTH)
        att = _attention(z3, lay['scal'], lay['qg'], lay['kg'], lay['lamp'], lay['sg'], seq_len)
        yl = _lru(z3, lay['conv_w'], lay['conv_b'], lay['wcat'], lay['ba'], lay['bx'],
                  lay['lam'], seq_len)
        h2 = _merge(att.reshape(b * lp, D_MODEL), yl.reshape(b * lp, D_MODEL), z2, h2,
                    lay['wa'], lay['wl'], lay['wo'], lay['bga'], lay['bgr'])
        if 'router' in lay:
            h2 = _moe(h2, lay['norm_ffn'], lay['router'], lay['wg'], lay['wu'], lay['wd'])
        else:
            h2 = _ffn(h2, lay['norm_ffn'], lay['wg'], lay['wu'], lay['wd'])
    return h2.reshape(b, lp, D_MODEL)[:, N_META:seq_len]


def kernel(x_prompt, x_sample, meta_tokens, norm_mix, norm_ffn, w_in, b_gate, q_norm, k_norm, lambda_q1, lambda_k1, lambda_q2, lambda_k2, attn_subln, w_attn_branch, conv_w, conv_b, lru_wa, lru_ba, lru_wx, lru_bx, lru_lambda, w_lru_branch, w_out, ffn_w_gate, ffn_w_up, ffn_w_down, moe_router, moe_w_gate, moe_w_up, moe_w_down):
    p = dict(norm_mix=norm_mix, norm_ffn=norm_ffn, w_in=w_in, b_gate=b_gate, q_norm=q_norm,
             k_norm=k_norm, lambda_q1=lambda_q1, lambda_k1=lambda_k1, lambda_q2=lambda_q2,
             lambda_k2=lambda_k2, attn_subln=attn_subln, w_attn_branch=w_attn_branch,
             conv_w=conv_w, conv_b=conv_b, lru_wa=lru_wa, lru_ba=lru_ba, lru_wx=lru_wx,
             lru_bx=lru_bx, lru_lambda=lru_lambda, w_lru_branch=w_lru_branch, w_out=w_out,
             ffn_w_gate=ffn_w_gate, ffn_w_up=ffn_w_up, ffn_w_down=ffn_w_down,
             moe_router=moe_router, moe_w_gate=moe_w_gate, moe_w_up=moe_w_up,
             moe_w_down=moe_w_down)
    layers = _prepare(p)
    return (_trunk(x_prompt, meta_tokens, layers), _trunk(x_sample, meta_tokens, layers))
```

```python
import functools
import math

import jax
import jax.numpy as jnp
from jax import lax
from jax.experimental import pallas as pl
from jax.experimental.pallas import tpu as pltpu

F32 = jnp.float32
BF16 = jnp.bfloat16

D_MODEL = 1024
N_META = 16
N_HEADS = 8
HEAD_DIM = 64
V_DIM = 2 * HEAD_DIM
LRU_BLOCKS = 16
LRU_BLOCK_W = D_MODEL // LRU_BLOCKS
LRU_C = 8.0
N_EXPERTS = 8
EPS = 1e-6
IN_WIDTH = 7 * D_MODEL

LANE = 128
SUBLANE = 8
VMEM_LIMIT_BYTES = 48 * 1024 * 1024
NEG_BIG = -1e30

ATT_TILE = 512
ATT_MAX_UNROLL = 8
EXP_ZERO_LOGIT = 105.0
VT_ROWS = V_DIM + 16
ATT_TAIL = LANE
ATT_VMEM_LIMIT_BYTES = 56 * 1024 * 1024
MASK_LOGIT = -30000.0
SAFE_LOGIT_SPAN = 80.0

TOP_K = 2
MOE_TM = 1024
MOE_TF = 512
MOE_DMA_UNROLL = 8

LRU_CB = 128
LRU_TT = 128


def _largest_tile(n, target, mult=LANE):
    assert n % mult == 0, (n, mult)
    best = mult
    t = mult
    while t <= min(n, target):
        if n % t == 0:
            best = t
        t += mult
    return best


def _cparams(*sem):
    return pltpu.CompilerParams(dimension_semantics=sem, vmem_limit_bytes=VMEM_LIMIT_BYTES)


def _rmsnorm_rows(x, g):
    ms = jnp.mean(x * x, axis=-1, keepdims=True)
    return x * lax.rsqrt(ms + EPS) * g


def _in_proj_kernel(h_ref, g_ref, w_ref, z_ref, xn_ref):
    @pl.when(pl.program_id(1) == 0)
    def _():
        xn_ref[...] = _rmsnorm_rows(h_ref[...], g_ref[...]).astype(BF16)

    z_ref[...] = jnp.dot(xn_ref[...], w_ref[...], preferred_element_type=F32)


def _in_proj(h2, g, w):
    m = h2.shape[0]
    tm = _largest_tile(m, 1024)
    tn = _largest_tile(IN_WIDTH, 1792, 2 * LANE)
    return pl.pallas_call(
        _in_proj_kernel,
        grid=(m // tm, IN_WIDTH // tn),
        in_specs=[
            pl.BlockSpec((tm, D_MODEL), lambda i, j: (i, 0)),
            pl.BlockSpec((1, D_MODEL), lambda i, j: (0, 0)),
            pl.BlockSpec((D_MODEL, tn), lambda i, j: (0, j)),
        ],
        out_specs=pl.BlockSpec((tm, tn), lambda i, j: (i, j)),
        out_shape=jax.ShapeDtypeStruct((m, IN_WIDTH), F32),
        scratch_shapes=[pltpu.VMEM((tm, D_MODEL), BF16)],
        compiler_params=_cparams("parallel", "arbitrary"),
        name="in_proj",
    )(h2, g, w)


def _attn_kernel(sc_ref, q_ref, k_ref, v_ref, qg_ref, kg_ref, lam_ref, sg_ref, o_ref,
                 kb_ref, vb_ref, m_ref, l_ref, acc_ref, *, seq_len, tq, tk):
    lp = k_ref.shape[0]
    nk = lp // tk
    head = pl.program_id(1)
    qi = pl.program_id(2)
    lane = lax.broadcasted_iota(jnp.int32, (1, LANE), 1)
    lo = lane < HEAD_DIM

    def half_rmsnorm(x, g):
        sq = x * x
        s_lo = jnp.sum(jnp.where(lo, sq, 0.0), axis=-1, keepdims=True)
        s_hi = jnp.sum(jnp.where(lo, 0.0, sq), axis=-1, keepdims=True)
        ms = jnp.where(lo, s_lo, s_hi) * (1.0 / HEAD_DIM)
        return x * lax.rsqrt(ms + EPS) * g

    @pl.when(qi == 0)
    def _():
        def fill(c, carry):
            r = pl.ds(pl.multiple_of(c * tk, tk), tk)
            kb_ref[r, :] = half_rmsnorm(k_ref[r, :], kg_ref[...]).astype(BF16)
            vb_ref[r, :] = v_ref[r, :].astype(BF16)
            return carry
        lax.fori_loop(0, nk, fill, 0)

    qn = half_rmsnorm(q_ref[...], qg_ref[...]) * (1.0 / math.sqrt(HEAD_DIM))
    q_comp = (jnp.where(lo, qn, 0.0).astype(BF16), jnp.where(lo, 0.0, qn).astype(BF16))

    m_ref[...] = jnp.full(m_ref.shape, NEG_BIG, F32)
    l_ref[...] = jnp.zeros(l_ref.shape, F32)
    acc_ref[...] = jnp.zeros(acc_ref.shape, F32)

    slope = sc_ref[head]
    rel = (lax.broadcasted_iota(jnp.int32, (tq, tk), 0)
           - lax.broadcasted_iota(jnp.int32, (tq, tk), 1)).astype(F32)
    col = lax.broadcasted_iota(jnp.int32, (1, tk), 1)

    def chunk(kc, carry):
        k0 = pl.multiple_of(kc * tk, tk)
        kblk = kb_ref[pl.ds(k0, tk), :]
        vblk = vb_ref[pl.ds(k0, tk), :]
        off = (qi * tq - k0).astype(F32)
        bias = -slope * jnp.abs(rel + off) + jnp.where(col + k0 < seq_len, 0.0, NEG_BIG)
        for c in range(2):
            s = lax.dot_general(q_comp[c], kblk, (((1,), (1,)), ((), ())),
                                preferred_element_type=F32) + bias
            m_prev = m_ref[c]
            m_new = jnp.maximum(m_prev, jnp.max(s, axis=-1, keepdims=True))
            alpha = jnp.exp(m_prev - m_new)
            p = jnp.exp(s - m_new)
            l_ref[c] = alpha * l_ref[c] + jnp.sum(p, axis=-1, keepdims=True)
            acc_ref[c] = alpha * acc_ref[c] + jnp.dot(p.astype(BF16), vblk,
                                                      preferred_element_type=F32)
            m_ref[c] = m_new
        return carry

    lax.fori_loop(0, nk, chunk, 0)

    lam_init = sc_ref[N_HEADS]
    lp_ = lam_ref[...]
    lam = (jnp.exp(jnp.sum(lp_[0:1] * lp_[1:2], axis=-1, keepdims=True))
           - jnp.exp(jnp.sum(lp_[2:3] * lp_[3:4], axis=-1, keepdims=True)) + lam_init)
    o = acc_ref[0] / l_ref[0] - lam * (acc_ref[1] / l_ref[1])
    o = _rmsnorm_rows(o, sg_ref[...]) * (1.0 - lam_init)
    o_ref[...] = o.astype(BF16)


def _attn_bounded_kernel(sc_ref, q_ref, k_ref, v_ref, qg_ref, kg_ref, lam_ref, sg_ref, prev_ref,
                         o_ref, kq_ref, vt_ref, vtail_ref, lhs_ref, bias_ref, acc_ref,
                         *, seq_len, t, head0, reach):
    lp = k_ref.shape[0]
    s_main = lp - ATT_TAIL
    n_main = s_main // t
    del prev_ref
    slope = sc_ref[head0 + pl.program_id(1)]
    lam_init = sc_ref[N_HEADS]
    lane = lax.broadcasted_iota(jnp.int32, (1, LANE), 1)
    lo = lane < HEAD_DIM
    nt = (((1,), (1,)), ((), ()))

    def half_rmsnorm(x, g):
        sq = x * x
        s_lo = jnp.sum(jnp.where(lo, sq, 0.0), axis=-1, keepdims=True)
        s_hi = jnp.sum(jnp.where(lo, 0.0, sq), axis=-1, keepdims=True)
        ms = jnp.where(lo, s_lo, s_hi) * (1.0 / HEAD_DIM)
        return x * lax.rsqrt(ms + EPS) * g

    def split2(x):
        hi = x.astype(BF16).astype(F32)
        return hi, x - hi

    def pos_col(r0, n):
        return lax.broadcasted_iota(jnp.int32, (n, 1), 0) + r0

    lp_ = lam_ref[...]
    lam = (jnp.exp(jnp.sum(lp_[0:1] * lp_[1:2], axis=-1, keepdims=True))
           - jnp.exp(jnp.sum(lp_[2:3] * lp_[3:4], axis=-1, keepdims=True)) + lam_init)

    ones_row = jnp.where(lax.broadcasted_iota(jnp.int32, (VT_ROWS - V_DIM, 1), 0) == 0, 1.0, 0.0)

    def fill(r0, n, vt_dst):
        rows = pos_col(r0, n)
        valid = rows < seq_len
        kn = jnp.where(valid, half_rmsnorm(k_ref[pl.ds(r0, n), :], kg_ref[...]), 0.0)
        sj_hi, sj_lo = split2(slope * rows.astype(F32))
        aug = jnp.where(lane == 0, -1.0,
              jnp.where((lane == 1) | (lane == 2), 1.0,
              jnp.where(lane == 3, sj_hi,
              jnp.where(lane == 4, sj_lo,
              jnp.where(lane == 5, jnp.where(valid, 0.0, MASK_LOGIT), 0.0)))))
        kq_ref[pl.ds(r0, n), 0:LANE] = kn.astype(BF16)
        kq_ref[pl.ds(r0, n), LANE:2 * LANE] = aug.astype(BF16)
        vt = jnp.transpose(jnp.where(valid, v_ref[pl.ds(r0, n), :], 0.0))
        vt_dst[...] = jnp.concatenate(
            [vt, jnp.broadcast_to(ones_row, (VT_ROWS - V_DIM, n))], axis=0).astype(BF16)

    def fill_chunk(c, carry):
        fill(pl.multiple_of(c * t, t), t, vt_ref.at[c])
        return carry
    lax.fori_loop(0, n_main, fill_chunk, 0, unroll=2)
    fill(s_main, ATT_TAIL, vtail_ref)

    def comp_max(g, c):
        return jnp.max(jnp.where(lo == (c == 0), jnp.abs(g), 0.0), axis=-1, keepdims=True)
    bounds = [math.sqrt(HEAD_DIM) * comp_max(qg_ref[...], c) * comp_max(kg_ref[...], c)
              for c in range(2)]

    rel = (lax.broadcasted_iota(jnp.int32, (t, t), 0)
           - lax.broadcasted_iota(jnp.int32, (t, t), 1)).astype(F32)
    bias_ref[0] = jnp.zeros((t, t), F32)
    bias_ref[1] = -slope * jnp.abs(rel)

    def build_lhs(r0, n):
        rows = pos_col(r0, n)
        qn = half_rmsnorm(q_ref[pl.ds(r0, n), :], qg_ref[...]) * (1.0 / math.sqrt(HEAD_DIM))
        si_hi, si_lo = split2(slope * rows.astype(F32))
        side = jnp.where(lane == 1, si_hi,
               jnp.where(lane == 2, si_lo,
               jnp.where((lane == 3) | (lane == 4), -1.0, 0.0)))
        for c in range(2):
            qc = jnp.where(lo, qn, 0.0) if c == 0 else jnp.where(lo, 0.0, qn)
            base = jnp.broadcast_to(
                jnp.where(lane == 0, bounds[c], jnp.where(lane == 5, 1.0, 0.0)), (n, LANE))
            qcb = qc.astype(BF16)
            for x, aug in enumerate((base - side, base, base + side)):
                lhs_ref[x, c * n:(c + 1) * n, 0:LANE] = qcb
                lhs_ref[x, c * n:(c + 1) * n, LANE:2 * LANE] = aug.astype(BF16)

    def process(x, n, k0, nk, vt, bias):
        s = lax.dot_general(kq_ref[pl.ds(k0, nk), :], lhs_ref[x, 0:2 * n, :], nt,
                            preferred_element_type=F32)
        if bias is not None:
            s = s + jnp.concatenate([bias, bias], axis=1)
        p = jnp.exp(s).astype(BF16)
        acc_ref[:, 0:2 * n] += jnp.dot(vt, p, preferred_element_type=F32)

    def finalize(r0, n):
        a0 = acc_ref[:, 0:n]
        a1 = acc_ref[:, n:2 * n]
        o = (a0[0:V_DIM] * (1.0 / a0[V_DIM:V_DIM + 1])
             - (lam * (1.0 / a1[V_DIM:V_DIM + 1])) * a1[0:V_DIM])
        ms = jnp.mean(o * o, axis=0, keepdims=True)
        o = o * lax.rsqrt(ms + EPS) * sg_ref[...] * (1.0 - lam_init)
        pos = lax.broadcasted_iota(jnp.int32, (1, n), 1) + r0
        o = jnp.where(pos < seq_len, o, 0.0)
        o_ref[pl.ds(r0, n), :] = jnp.transpose(o).astype(BF16)

    window = min(2 * reach + 1, n_main)

    def full_chunks(qi, n, kc0, count):
        def body(w, carry):
            kc = kc0 + w
            x = jnp.where(kc < qi, 0, jnp.where(kc == qi, 1, 2))
            d = jnp.where(kc == qi, 1, 0)
            process(x, n, pl.multiple_of(kc * t, t), t, vt_ref[kc], bias_ref[d, :, 0:n])
            return carry
        lax.fori_loop(0, count, body, 0, unroll=min(count, ATT_MAX_UNROLL))

    def q_main(qi, carry):
        r0 = pl.multiple_of(qi * t, t)
        build_lhs(r0, t)
        acc_ref[...] = jnp.zeros(acc_ref.shape, F32)
        full_chunks(qi, t, jnp.clip(qi - reach, 0, n_main - window), window)

        @pl.when(n_main - qi <= reach)
        def _():
            process(2, t, s_main, ATT_TAIL, vtail_ref[...], None)
        finalize(r0, t)
        return carry
    lax.fori_loop(0, n_main, q_main, 0)

    build_lhs(s_main, ATT_TAIL)
    acc_ref[...] = jnp.zeros(acc_ref.shape, F32)
    tail_count = min(reach, n_main)
    full_chunks(n_main, ATT_TAIL, n_main - tail_count, tail_count)
    process(1, ATT_TAIL, s_main, ATT_TAIL, vtail_ref[...], bias_ref[1, 0:ATT_TAIL, 0:ATT_TAIL])
    finalize(s_main, ATT_TAIL)


def _chunk_reach(head, t):
    slope = 2.0 ** (-8.0 * (head + 1) / N_HEADS)
    return max(1, math.ceil((EXP_ZERO_LOGIT / slope - 1.0) / t))


def _attention_bounded(z3, scal, qg, kg, lamp, sg, seq_len):
    b, lp, _ = z3.shape
    t = _largest_tile(lp - ATT_TAIL, ATT_TILE)
    n_main = (lp - ATT_TAIL) // t
    groups = []
    for head in range(N_HEADS):
        reach = min(_chunk_reach(head, t), n_main)
        if groups and min(2 * groups[-1][2] + 1, n_main) == min(2 * reach + 1, n_main):
            groups[-1][1] += 1
            groups[-1][2] = max(groups[-1][2], reach)
        else:
            groups.append([head, 1, reach])
    small = lambda shape: pl.BlockSpec(shape, lambda bi, hi: (0, 0))
    att = jnp.zeros((b, lp, N_HEADS * V_DIM), BF16)
    for head0, count, reach in groups:
        kern = functools.partial(_attn_bounded_kernel, seq_len=seq_len, t=t, head0=head0,
                                 reach=reach)
        col = lambda off, head0=head0: pl.BlockSpec(
            (None, lp, V_DIM), lambda bi, hi: (bi, 0, off + head0 + hi))
        att = pl.pallas_call(
            kern,
            grid=(b, count),
            in_specs=[
                pl.BlockSpec(memory_space=pltpu.SMEM),
                col(0), col(N_HEADS), col(2 * N_HEADS),
                small((1, V_DIM)), small((1, V_DIM)), small((4, HEAD_DIM)), small((V_DIM, 1)),
                pl.BlockSpec(memory_space=pl.ANY),
            ],
            out_specs=col(0),
            out_shape=jax.ShapeDtypeStruct(att.shape, att.dtype),
            input_output_aliases={8: 0},
            scratch_shapes=[
                pltpu.VMEM((lp, 2 * LANE), BF16),
                pltpu.VMEM((n_main, VT_ROWS, t), BF16), pltpu.VMEM((VT_ROWS, ATT_TAIL), BF16),
                pltpu.VMEM((3, 2 * t, 2 * LANE), BF16),
                pltpu.VMEM((2, t, t), F32),
                pltpu.VMEM((VT_ROWS, 2 * t), F32),
            ],
            compiler_params=pltpu.CompilerParams(dimension_semantics=("parallel", "parallel"),
                                                 vmem_limit_bytes=ATT_VMEM_LIMIT_BYTES),
            name="diff_attention_bounded",
        )(scal, z3, z3, z3, qg, kg, lamp, sg.reshape(V_DIM, 1), att)
    return att


def _attention(z3, scal, qg, kg, lamp, sg, seq_len):
    span = 2.0 * HEAD_DIM / math.sqrt(HEAD_DIM) * jnp.max(jnp.abs(qg)) * jnp.max(jnp.abs(kg))
    args = (z3, scal, qg, kg, lamp, sg)
    return lax.cond(span <= SAFE_LOGIT_SPAN,
                    lambda *a: _attention_bounded(*a, seq_len),
                    lambda *a: _attention_online(*a, seq_len), *args)


def _attention_online(z3, scal, qg, kg, lamp, sg, seq_len):
    b, lp, _ = z3.shape
    tq = _largest_tile(lp, 768)
    tk = tq
    kern = functools.partial(_attn_kernel, seq_len=seq_len, tq=tq, tk=tk)
    small = lambda shape: pl.BlockSpec(shape, lambda bi, hi, qi: (0, 0))
    return pl.pallas_call(
        kern,
        grid=(b, N_HEADS, lp // tq),
        in_specs=[
            pl.BlockSpec(memory_space=pltpu.SMEM),
            pl.BlockSpec((None, tq, V_DIM), lambda bi, hi, qi: (bi, qi, hi)),
            pl.BlockSpec((None, lp, V_DIM), lambda bi, hi, qi: (bi, 0, N_HEADS + hi)),
            pl.BlockSpec((None, lp, V_DIM), lambda bi, hi, qi: (bi, 0, 2 * N_HEADS + hi)),
            small((1, V_DIM)), small((1, V_DIM)), small((4, HEAD_DIM)), small((1, V_DIM)),
        ],
        out_specs=pl.BlockSpec((None, tq, V_DIM), lambda bi, hi, qi: (bi, qi, hi)),
        out_shape=jax.ShapeDtypeStruct((b, lp, N_HEADS * V_DIM), BF16),
        scratch_shapes=[
            pltpu.VMEM((lp, V_DIM), BF16), pltpu.VMEM((lp, V_DIM), BF16),
            pltpu.VMEM((2, tq, 1), F32), pltpu.VMEM((2, tq, 1), F32),
            pltpu.VMEM((2, tq, V_DIM), F32),
        ],
        compiler_params=_cparams("parallel", "parallel", "arbitrary"),
        name="diff_attention",
    )(scal, z3, z3, z3, qg, kg, lamp, sg)


def _lru_kernel(xl_ref, gl_ref, cw_ref, cbias_ref, w_ref, ba_ref, bx_ref, lam_ref, o_ref,
                xp_ref, hf_ref, hb_ref, *, seq_len, tt):
    lp, cb = xl_ref.shape
    n = lp // tt
    pad = 8
    rows_t = lax.broadcasted_iota(jnp.int32, (tt, 1), 0)

    xp_ref[0:pad, :] = jnp.zeros((pad, cb), F32)
    xp_ref[lp + pad:lp + 2 * pad, :] = jnp.zeros((pad, cb), F32)

    def copy_in(c, carry):
        r0 = pl.multiple_of(c * tt, tt)
        x = xl_ref[pl.ds(r0, tt), :]
        xp_ref[pl.ds(r0 + pad, tt), :] = jnp.where(rows_t + r0 < seq_len, x, 0.0)
        return carry
    lax.fori_loop(0, n, copy_in, 0)

    cw = cw_ref[...]

    def gates(c, d):
        r0 = pl.multiple_of(c * tt, tt)
        win = xp_ref[pl.ds(r0, tt + 2 * pad), :]
        wn = tt + 2 * pad
        xc = (pltpu.roll(win, 2, 0)[pad:pad + tt] * cw[0:1]
              + pltpu.roll(win, 1, 0)[pad:pad + tt] * cw[1:2]
              + win[pad:pad + tt] * cw[2:3]
              + pltpu.roll(win, wn - 1, 0)[pad:pad + tt] * cw[3:4]
              + cbias_ref[...])
        pre = jnp.dot(xc.astype(BF16), w_ref[:, d * 2 * cb:(d + 1) * 2 * cb],
                      preferred_element_type=F32)
        r = jax.nn.sigmoid(pre[:, :cb] + ba_ref[d:d + 1, :])
        i = jax.nn.sigmoid(pre[:, cb:] + bx_ref[d:d + 1, :])
        nl = -lam_ref[d:d + 1, :]
        softplus = jnp.maximum(nl, 0.0) + jnp.log1p(jnp.exp(-jnp.abs(nl)))
        log_a = (-LRU_C * r) * softplus
        a = jnp.exp(log_a)
        th = jnp.tanh(log_a)
        m2 = -2.0 * th / (1.0 - th)
        mult = jnp.where(m2 > 0.0, m2 * lax.rsqrt(m2), 0.0)
        u = mult * (i * xc)
        u = jnp.where(rows_t + r0 < seq_len, u, 0.0)
        return a, u

    ng = tt // SUBLANE
    sub = lax.broadcasted_iota(jnp.int32, (1, SUBLANE, 1), 1)

    def scan_chunk(a, u, carry, reverse):
        a3 = a.reshape(ng, SUBLANE, cb)
        u3 = u.reshape(ng, SUBLANE, cb)
        d = 1
        while d < SUBLANE:
            if reverse:
                shift, keep = SUBLANE - d, sub < SUBLANE - d
            else:
                shift, keep = d, sub >= d
            u3 = u3 + a3 * jnp.where(keep, pltpu.roll(u3, shift, 1), 0.0)
            a3 = a3 * jnp.where(keep, pltpu.roll(a3, shift, 1), 1.0)
            d *= 2
        hs = [None] * ng
        for g in (range(ng - 1, -1, -1) if reverse else range(ng)):
            hg = u3[g] + a3[g] * carry
            hs[g] = hg
            carry = hg[0:1, :] if reverse else hg[SUBLANE - 1:SUBLANE, :]
        return jnp.concatenate(hs, axis=0), carry

    def both(j, carry):
        cf = j
        h, state_f = scan_chunk(*gates(cf, 0), carry[0], False)
        hf_ref[pl.ds(pl.multiple_of(cf * tt, tt), tt), :] = h
        cr = n - 1 - j
        h, state_b = scan_chunk(*gates(cr, 1), carry[1], True)
        hb_ref[pl.ds(pl.multiple_of(cr * tt, tt), tt), :] = h
        return state_f, state_b
    zero_state = jnp.zeros((1, cb), F32)
    lax.fori_loop(0, n, both, (zero_state, zero_state), unroll=2)

    def combine(c, carry):
        r = pl.ds(pl.multiple_of(c * tt, tt), tt)
        o_ref[r, :] = ((hf_ref[r, :] + hb_ref[r, :]) * jax.nn.gelu(gl_ref[r, :])).astype(BF16)
        return carry
    lax.fori_loop(0, n, combine, 0, unroll=2)


def _lru(z3, cw, cbias, wcat, ba, bx, lam, seq_len):
    b, lp, _ = z3.shape
    cb = LRU_CB
    nc = D_MODEL // cb
    kern = functools.partial(_lru_kernel, seq_len=seq_len, tt=LRU_TT)
    chan = lambda rows: pl.BlockSpec((rows, cb), lambda bi, ci: (0, ci))
    return pl.pallas_call(
        kern,
        grid=(b, nc),
        in_specs=[
            pl.BlockSpec((None, lp, cb), lambda bi, ci: (bi, 0, 3 * nc + ci)),
            pl.BlockSpec((None, lp, cb), lambda bi, ci: (bi, 0, 4 * nc + ci)),
            chan(4), chan(1),
            pl.BlockSpec((None, cb, 4 * cb), lambda bi, ci: (ci, 0, 0)),
            chan(2), chan(2), chan(2),
        ],
        out_specs=pl.BlockSpec((None, lp, cb), lambda bi, ci: (bi, 0, ci)),
        out_shape=jax.ShapeDtypeStruct((b, lp, D_MODEL), BF16),
        scratch_shapes=[pltpu.VMEM((lp + 16, cb), F32), pltpu.VMEM((lp, cb), F32),
                        pltpu.VMEM((lp, cb), F32)],
        compiler_params=_cparams("parallel", "parallel"),
        name="rglru",
    )(z3, z3, cw, cbias, wcat, ba, bx, lam)


def _merge_kernel(att_ref, yl_ref, ga_ref, gr_ref, h_ref, wa_ref, wl_ref, wo_ref,
                  bga_ref, bgr_ref, o_ref):
    ya = jnp.dot(att_ref[...], wa_ref[...], preferred_element_type=F32)
    yr = jnp.dot(yl_ref[...], wl_ref[...], preferred_element_type=F32)
    merged = (jax.nn.sigmoid(ga_ref[...] + bga_ref[...]) * ya
              + jax.nn.sigmoid(gr_ref[...] + bgr_ref[...]) * yr)
    o_ref[...] = h_ref[...] + jnp.dot(merged.astype(BF16), wo_ref[...],
                                      preferred_element_type=F32)


def _merge(att2, yl2, z2, h2, wa, wl, wo, bga, bgr):
    m = h2.shape[0]
    tm = _largest_tile(m, 512)
    row = lambda col: pl.BlockSpec((tm, D_MODEL), lambda i: (i, col))
    full = lambda shape: pl.BlockSpec(shape, lambda i: (0, 0))
    return pl.pallas_call(
        _merge_kernel,
        grid=(m // tm,),
        in_specs=[row(0), row(0), row(5), row(6), row(0),
                  full((D_MODEL, D_MODEL)), full((D_MODEL, D_MODEL)), full((D_MODEL, D_MODEL)),
                  full((1, D_MODEL)), full((1, D_MODEL))],
        out_specs=row(0),
        out_shape=jax.ShapeDtypeStruct((m, D_MODEL), F32),
        compiler_params=_cparams("parallel"),
        name="merge_out_proj",
    )(att2, yl2, z2, z2, h2, wa, wl, wo, bga, bgr)


def _ffn_kernel(h_ref, g_ref, wg_ref, wu_ref, wd_ref, o_ref, hn_ref, acc_ref):
    j = pl.program_id(1)

    @pl.when(j == 0)
    def _():
        h = h_ref[...]
        hn_ref[...] = _rmsnorm_rows(h, g_ref[...]).astype(BF16)
        acc_ref[...] = h

    hn = hn_ref[...]
    gg = jnp.dot(hn, wg_ref[...], preferred_element_type=F32)
    uu = jnp.dot(hn, wu_ref[...], preferred_element_type=F32)
    act = (jax.nn.silu(gg) * uu).astype(BF16)
    acc_ref[...] += jnp.dot(act, wd_ref[...], preferred_element_type=F32)

    @pl.when(j == pl.num_programs(1) - 1)
    def _():
        o_ref[...] = acc_ref[...]


def _ffn(h2, g, wg, wu, wd):
    m = h2.shape[0]
    f = wg.shape[1]
    tm = _largest_tile(m, 1024)
    tf = MOE_TF
    return pl.pallas_call(
        _ffn_kernel,
        grid=(m // tm, f // tf),
        in_specs=[pl.BlockSpec((tm, D_MODEL), lambda i, j: (i, 0)),
                  pl.BlockSpec((1, D_MODEL), lambda i, j: (0, 0)),
                  pl.BlockSpec((D_MODEL, tf), lambda i, j: (0, j)),
                  pl.BlockSpec((D_MODEL, tf), lambda i, j: (0, j)),
                  pl.BlockSpec((tf, D_MODEL), lambda i, j: (j, 0))],
        out_specs=pl.BlockSpec((tm, D_MODEL), lambda i, j: (i, 0)),
        out_shape=jax.ShapeDtypeStruct((m, D_MODEL), F32),
        scratch_shapes=[pltpu.VMEM((tm, D_MODEL), BF16), pltpu.VMEM((tm, D_MODEL), F32)],
        compiler_params=_cparams("parallel", "arbitrary"),
        name="dense_ffn",
    )(h2, g, wg, wu, wd)


def _moe_router_kernel(h_ref, g_ref, wr_ref, r_ref, cnt_ref, tri_ref, base_ref):
    tm = h_ref.shape[0]

    @pl.when(pl.program_id(0) == 0)
    def _():
        row = lax.broadcasted_iota(jnp.int32, (tm, tm), 0)
        col = lax.broadcasted_iota(jnp.int32, (tm, tm), 1)
        tri_ref[...] = jnp.where(col < row, 1.0, 0.0).astype(BF16)
        base_ref[...] = jnp.zeros(base_ref.shape, F32)

    hn = _rmsnorm_rows(h_ref[...], g_ref[...])
    logits = jnp.dot(hn, wr_ref[...], preferred_element_type=F32, precision=lax.Precision.HIGHEST)
    lane = lax.broadcasted_iota(jnp.int32, logits.shape, 1).astype(F32)
    lg = jnp.where(lane < N_EXPERTS, logits, NEG_BIG)
    m1 = jnp.max(lg, axis=-1, keepdims=True)
    i1 = jnp.min(jnp.where(lg == m1, lane, float(LANE)), axis=-1, keepdims=True)
    lg2 = jnp.where(lane == i1, NEG_BIG, lg)
    m2 = jnp.max(lg2, axis=-1, keepdims=True)
    i2 = jnp.min(jnp.where(lg2 == m2, lane, float(LANE)), axis=-1, keepdims=True)
    e2 = jnp.exp(m2 - m1)
    den = 1.0 + e2
    oh1 = lane == i1
    oh2 = lane == i2
    assigned = jnp.where(oh1 | oh2, 1.0, 0.0)
    before = jnp.dot(tri_ref[...], assigned.astype(BF16), preferred_element_type=F32) + base_ref[...]
    rank1 = jnp.sum(jnp.where(oh1, before, 0.0), axis=-1, keepdims=True)
    rank2 = jnp.sum(jnp.where(oh2, before, 0.0), axis=-1, keepdims=True)
    r_ref[...] = jnp.where(lane == 0, i1, jnp.where(lane == 1, i2,
                 jnp.where(lane == 2, 1.0 / den, jnp.where(lane == 3, e2 / den,
                 jnp.where(lane == 4, rank1, jnp.where(lane == 5, rank2, 0.0))))))
    base_ref[...] += jnp.sum(assigned, axis=0, keepdims=True)
    cnt_ref[...] = base_ref[...]


def _moe_router(h2, g, router):
    m = h2.shape[0]
    tm = _largest_tile(m, 1024)
    return pl.pallas_call(
        _moe_router_kernel,
        grid=(m // tm,),
        in_specs=[pl.BlockSpec((tm, D_MODEL), lambda i: (i, 0)),
                  pl.BlockSpec((1, D_MODEL), lambda i: (0, 0)),
                  pl.BlockSpec((D_MODEL, LANE), lambda i: (0, 0))],
        out_specs=[pl.BlockSpec((tm, LANE), lambda i: (i, 0)),
                   pl.BlockSpec((1, LANE), lambda i: (0, 0))],
        out_shape=[jax.ShapeDtypeStruct((m, LANE), F32), jax.ShapeDtypeStruct((1, LANE), F32)],
        scratch_shapes=[pltpu.VMEM((tm, tm), BF16), pltpu.VMEM((1, LANE), F32)],
        compiler_params=_cparams("arbitrary"),
        name="moe_router",
    )(h2, g, router)


def _slab_copy(src_ref, src_tok, dst_ref, dst_tok, sem):
    return pltpu.make_async_copy(
        src_ref.at[pl.ds(pl.multiple_of(src_tok * SUBLANE, SUBLANE), SUBLANE), :],
        dst_ref.at[pl.ds(pl.multiple_of(dst_tok * SUBLANE, SUBLANE), SUBLANE), :], sem)


def _rows_to_slabs(x, slab_ref):
    tm = x.shape[0]
    for s in range(D_MODEL // LANE):
        slab_ref[pl.ds(s, tm, stride=SUBLANE), :] = x[:, s * LANE:(s + 1) * LANE]


def _slabs_lane_block(slab_ref, s, tm):
    return slab_ref[pl.ds(s, tm, stride=SUBLANE), :]


def _moe_scatter_kernel(pos_ref, h_ref, g_ref, xs_in_ref, xs_ref, slab_ref, sem):
    del xs_in_ref
    tm = h_ref.shape[0]
    _rows_to_slabs(_rmsnorm_rows(h_ref[...], g_ref[...]), slab_ref)

    def body(r, carry):
        for s in range(TOP_K):
            _slab_copy(slab_ref, r, xs_ref, pos_ref[s, r], sem).start(priority=s)
        return carry
    lax.fori_loop(0, tm, body, 0, unroll=MOE_DMA_UNROLL)
    for s in range(TOP_K):
        pltpu.make_async_copy(slab_ref, xs_ref.at[pl.ds(0, tm * SUBLANE), :], sem).wait()


def _moe_scatter(pos, h2, g, n_rows):
    m = h2.shape[0]
    tm = _largest_tile(m, MOE_TM)
    xs0 = jnp.zeros((n_rows * SUBLANE, LANE), F32)
    return pl.pallas_call(
        _moe_scatter_kernel,
        grid=(m // tm,),
        in_specs=[pl.BlockSpec((TOP_K, tm), lambda i: (0, i), memory_space=pltpu.SMEM),
                  pl.BlockSpec((tm, D_MODEL), lambda i: (i, 0)),
                  pl.BlockSpec((1, D_MODEL), lambda i: (0, 0)),
                  pl.BlockSpec(memory_space=pl.ANY)],
        out_specs=pl.BlockSpec(memory_space=pl.ANY),
        out_shape=jax.ShapeDtypeStruct(xs0.shape, xs0.dtype),
        input_output_aliases={3: 0},
        scratch_shapes=[pltpu.VMEM((tm * SUBLANE, LANE), F32), pltpu.SemaphoreType.DMA(())],
        compiler_params=_cparams("arbitrary"),
        name="moe_scatter",
    )(pos, h2, g, xs0)


def _moe_expert_kernel(te_ref, x_ref, wg_ref, wu_ref, wd_ref, y_ref, xb_ref, acc_ref):
    del te_ref
    j = pl.program_id(1)
    tm = xb_ref.shape[0]

    @pl.when(j == 0)
    def _():
        for s in range(D_MODEL // LANE):
            xb_ref[:, s * LANE:(s + 1) * LANE] = _slabs_lane_block(x_ref, s, tm).astype(BF16)
        acc_ref[...] = jnp.zeros(acc_ref.shape, F32)

    xb = xb_ref[...]
    gg = jnp.dot(xb, wg_ref[...], preferred_element_type=F32)
    uu = jnp.dot(xb, wu_ref[...], preferred_element_type=F32)
    act = (jax.nn.silu(gg) * uu).astype(BF16)
    acc_ref[...] += jnp.dot(act, wd_ref[...], preferred_element_type=F32)

    @pl.when(j == pl.num_programs(1) - 1)
    def _():
        _rows_to_slabs(acc_ref[...], y_ref)


def _moe_experts(tile_expert, xs, wg, wu, wd):
    n_rows = xs.shape[0] // SUBLANE
    fe = wg.shape[2]
    tm, tf = MOE_TM, MOE_TF
    slab_tile = pl.BlockSpec((tm * SUBLANE, LANE), lambda i, j, te: (i, 0))
    grid_spec = pltpu.PrefetchScalarGridSpec(
        num_scalar_prefetch=1,
        grid=(n_rows // tm, fe // tf),
        in_specs=[slab_tile,
                  pl.BlockSpec((None, D_MODEL, tf), lambda i, j, te: (te[i], 0, j)),
                  pl.BlockSpec((None, D_MODEL, tf), lambda i, j, te: (te[i], 0, j)),
                  pl.BlockSpec((None, tf, D_MODEL), lambda i, j, te: (te[i], j, 0))],
        out_specs=slab_tile,
        scratch_shapes=[pltpu.VMEM((tm, D_MODEL), BF16), pltpu.VMEM((tm, D_MODEL), F32)])
    return pl.pallas_call(
        _moe_expert_kernel,
        grid_spec=grid_spec,
        out_shape=jax.ShapeDtypeStruct(xs.shape, F32),
        compiler_params=_cparams("parallel", "arbitrary"),
        name="moe_experts",
    )(tile_expert, xs, wg, wu, wd)


def _moe_combine_kernel(pos_ref, r_ref, h_ref, ys_ref, o_ref, y0_ref, y1_ref, sem):
    tm = h_ref.shape[0]
    bufs = (y0_ref, y1_ref)

    def body(r, carry):
        for s in range(TOP_K):
            _slab_copy(ys_ref, pos_ref[s, r], bufs[s], r, sem).start(priority=s)
        return carry
    lax.fori_loop(0, tm, body, 0, unroll=MOE_DMA_UNROLL)
    for s in range(TOP_K):
        pltpu.make_async_copy(ys_ref.at[pl.ds(0, tm * SUBLANE), :], bufs[s], sem).wait()
    r = r_ref[...]
    for s in range(D_MODEL // LANE):
        o_ref[:, s * LANE:(s + 1) * LANE] = h_ref[:, s * LANE:(s + 1) * LANE] + (
            r[:, 2:3] * _slabs_lane_block(y0_ref, s, tm) + r[:, 3:4] * _slabs_lane_block(y1_ref, s, tm))


def _moe_combine(pos, route, h2, ys):
    m = h2.shape[0]
    tm = _largest_tile(m, MOE_TM)
    return pl.pallas_call(
        _moe_combine_kernel,
        grid=(m // tm,),
        in_specs=[pl.BlockSpec((TOP_K, tm), lambda i: (0, i), memory_space=pltpu.SMEM),
                  pl.BlockSpec((tm, LANE), lambda i: (i, 0)),
                  pl.BlockSpec((tm, D_MODEL), lambda i: (i, 0)),
                  pl.BlockSpec(memory_space=pl.ANY)],
        out_specs=pl.BlockSpec((tm, D_MODEL), lambda i: (i, 0)),
        out_shape=jax.ShapeDtypeStruct((m, D_MODEL), F32),
        scratch_shapes=[pltpu.VMEM((tm * SUBLANE, LANE), F32), pltpu.VMEM((tm * SUBLANE, LANE), F32),
                        pltpu.SemaphoreType.DMA(())],
        compiler_params=_cparams("arbitrary"),
        name="moe_combine",
    )(pos, route, h2, ys)


def _moe(h2, g, router, wg, wu, wd):
    m = h2.shape[0]
    route, counts = _moe_router(h2, g, router)
    cnt = counts[0, :N_EXPERTS].astype(jnp.int32)
    padded = (cnt + MOE_TM - 1) // MOE_TM * MOE_TM
    ends = jnp.cumsum(padded)
    starts = ends - padded
    expert = route[:, 0:TOP_K].astype(jnp.int32)
    pos = (starts[expert] + route[:, 4:4 + TOP_K].astype(jnp.int32)).T
    n_rows = TOP_K * m + N_EXPERTS * MOE_TM
    tile_start = jnp.arange(n_rows // MOE_TM, dtype=jnp.int32) * MOE_TM
    tile_expert = jnp.minimum(jnp.searchsorted(ends, tile_start, side='right'),
                              N_EXPERTS - 1).astype(jnp.int32)
    xs = _moe_scatter(pos, h2, g, n_rows)
    ys = _moe_experts(tile_expert, xs, wg, wu, wd)
    return _moe_combine(pos, route, h2, ys)


def _blockdiag_groups(w, per):
    g = LRU_BLOCKS // per
    w = w.reshape(g, per, LRU_BLOCK_W, LRU_BLOCK_W)
    eye = jnp.eye(per, dtype=w.dtype)
    out = jnp.einsum('gpio,pq->gpiqo', w, eye)
    return out.reshape(g, per * LRU_BLOCK_W, per * LRU_BLOCK_W)


def _prepare(p):
    depth = p['w_in'].shape[0]
    per = LRU_CB // LRU_BLOCK_W
    layers = []
    slopes = jnp.exp2(-8.0 * jnp.arange(1, N_HEADS + 1, dtype=F32) / N_HEADS)
    for l in range(depth):
        lam_init = 0.8 - 0.6 * math.exp(-0.3 * l)
        wcat = jnp.concatenate(
            [_blockdiag_groups(p['lru_wa'][l, 0], per), _blockdiag_groups(p['lru_wx'][l, 0], per),
             _blockdiag_groups(p['lru_wa'][l, 1], per), _blockdiag_groups(p['lru_wx'][l, 1], per)],
            axis=-1).astype(BF16)
        lay = dict(
            norm_mix=p['norm_mix'][l][None], norm_ffn=p['norm_ffn'][l][None],
            w_in=p['w_in'][l].astype(BF16),
            scal=jnp.concatenate([slopes, jnp.full((N_HEADS,), lam_init, F32)]),
            qg=jnp.tile(p['q_norm'][l], 2)[None], kg=jnp.tile(p['k_norm'][l], 2)[None],
            lamp=jnp.stack([p['lambda_q1'][l], p['lambda_k1'][l],
                            p['lambda_q2'][l], p['lambda_k2'][l]]),
            sg=p['attn_subln'][l][None],
            conv_w=p['conv_w'][l], conv_b=p['conv_b'][l][None], wcat=wcat,
            ba=p['lru_ba'][l], bx=p['lru_bx'][l], lam=p['lru_lambda'][l],
            wa=p['w_attn_branch'][l].astype(BF16), wl=p['w_lru_branch'][l].astype(BF16),
            wo=p['w_out'][l].astype(BF16),
            bga=p['b_gate'][l, :D_MODEL][None], bgr=p['b_gate'][l, D_MODEL:][None],
        )
        j = l // 2
        if l % 2 == 0:
            lay.update(wg=p['ffn_w_gate'][j].astype(BF16), wu=p['ffn_w_up'][j].astype(BF16),
                       wd=p['ffn_w_down'][j].astype(BF16))
        else:
            lay.update(
                wg=p['moe_w_gate'][j].astype(BF16), wu=p['moe_w_up'][j].astype(BF16),
                wd=p['moe_w_down'][j].astype(BF16),
                router=jnp.pad(p['moe_router'][j], ((0, 0), (0, LANE - N_EXPERTS))))
        layers.append(lay)
    return layers


def _trunk(x, meta, layers):
    b, s, _ = x.shape
    seq_len = s + N_META
    lp = -(-seq_len // LANE) * LANE
    h = jnp.concatenate(
        [jnp.broadcast_to(meta[None].astype(x.dtype), (b, N_META, D_MODEL)), x,
         jnp.zeros((b, lp - seq_len, D_MODEL), x.dtype)], axis=1)
    h2 = h.reshape(b * lp, D_MODEL)
    for lay in layers:
        z2 = _in_proj(h2, lay['norm_mix'], lay['w_in'])
        z3 = z2.reshape(b, lp, IN_WIDTH)
        att = _attention(z3, lay['scal'], lay['qg'], lay['kg'], lay['lamp'], lay['sg'], seq_len)
        yl = _lru(z3, lay['conv_w'], lay['conv_b'], lay['wcat'], lay['ba'], lay['bx'],
                  lay['lam'], seq_len)
        h2 = _merge(att.reshape(b * lp, D_MODEL), yl.reshape(b * lp, D_MODEL), z2, h2,
                    lay['wa'], lay['wl'], lay['wo'], lay['bga'], lay['bgr'])
        if 'router' in lay:
            h2 = _moe(h2, lay['norm_ffn'], lay['router'], lay['wg'], lay['wu'], lay['wd'])
        else:
            h2 = _ffn(h2, lay['norm_ffn'], lay['wg'], lay['wu'], lay['wd'])
    return h2.reshape(b, lp, D_MODEL)[:, N_META:seq_len]


def kernel(x_prompt, x_sample, meta_tokens, norm_mix, norm_ffn, w_in, b_gate, q_norm, k_norm, lambda_q1, lambda_k1, lambda_q2, lambda_k2, attn_subln, w_attn_branch, conv_w, conv_b, lru_wa, lru_ba, lru_wx, lru_bx, lru_lambda, w_lru_branch, w_out, ffn_w_gate, ffn_w_up, ffn_w_down, moe_router, moe_w_gate, moe_w_up, moe_w_down):
    p = dict(norm_mix=norm_mix, norm_ffn=norm_ffn, w_in=w_in, b_gate=b_gate, q_norm=q_norm,
             k_norm=k_norm, lambda_q1=lambda_q1, lambda_k1=lambda_k1, lambda_q2=lambda_q2,
             lambda_k2=lambda_k2, attn_subln=attn_subln, w_attn_branch=w_attn_branch,
             conv_w=conv_w, conv_b=conv_b, lru_wa=lru_wa, lru_ba=lru_ba, lru_wx=lru_wx,
             lru_bx=lru_bx, lru_lambda=lru_lambda, w_lru_branch=w_lru_branch, w_out=w_out,
             ffn_w_gate=ffn_w_gate, ffn_w_up=ffn_w_up, ffn_w_down=ffn_w_down,
             moe_router=moe_router, moe_w_gate=moe_w_gate, moe_w_up=moe_w_up,
             moe_w_down=moe_w_down)
    layers = _prepare(p)
    return (_trunk(x_prompt, meta_tokens, layers), _trunk(x_sample, meta_tokens, layers))
```
